```python
import jax, jax.numpy as jnp
from jax import lax
import numpy as np

D_MODEL = 1024
BATCH = 2
SEQ = 16384
DEPTH = 2

RMS_EPS = 1e-6
BLOCK = 128
DIL_PAIRS = ((128, 1), (512, 4), (2048, 16))
N_DIL_GROUPS = 3
DSA_HEADS = 8
DSA_HEAD_DIM = 64
DSA_WIDTH = DSA_HEADS * DSA_HEAD_DIM
DSA_QKV_COLS = N_DIL_GROUPS * 3 * DSA_WIDTH
MLA_HEADS = 16
QK_NOPE = 64
QK_ROPE = 32
V_DIM = 64
Q_LORA = 384
KV_LORA = 256
MLA_IN_COLS = Q_LORA + KV_LORA + QK_ROPE
ROPE_THETA = 10000.0
D_FF_DENSE = 2816
N_EXPERTS = 8
TOP_K = 2
D_FF_EXPERT = 3584

kernel_name = "hybrid_dilated_mla_moe_trunk"


def rmsnorm(x, g):
    xf = x.astype(jnp.float32)
    y = xf * lax.rsqrt(jnp.mean(xf * xf, axis=-1, keepdims=True) + RMS_EPS)
    return (y * g.astype(jnp.float32)).astype(x.dtype)


def alibi_slopes(n):
    return jnp.exp2(-8.0 * (jnp.arange(n, dtype=jnp.float32) + 1.0) / n)


def dilated_window_attention(q, k, v, pos, dilation, span, slopes):
    B, S, H, Dh = q.shape
    L = S // dilation
    nb = -(-L // BLOCK)
    Lp = nb * BLOCK

    def to_sub(a):
        a = jnp.swapaxes(a.reshape((B, L, dilation) + a.shape[2:]), 1, 2)
        return jnp.pad(a, [(0, 0), (0, 0), (0, Lp - L)] + [(0, 0)] * (a.ndim - 3))

    def band(a):
        a = jnp.pad(a, [(0, 0), (0, 0), (BLOCK, 0)] + [(0, 0)] * (a.ndim - 3))
        a = a.reshape((B, dilation, nb + 1, BLOCK) + a.shape[3:])
        return jnp.concatenate([a[:, :, :-1], a[:, :, 1:]], axis=3)

    qs, ks, vs, ps = to_sub(q), to_sub(k), to_sub(v), to_sub(pos)
    qb = qs.reshape(B, dilation, nb, BLOCK, H, Dh)
    kb, vb, pk = band(ks), band(vs), band(ps)
    pq = ps.reshape(B, dilation, nb, BLOCK)

    rel = jnp.arange(BLOCK)[:, None] + BLOCK - jnp.arange(2 * BLOCK)[None, :]
    key_idx = jnp.arange(nb)[:, None] * BLOCK + jnp.arange(2 * BLOCK)[None, :] - BLOCK
    valid = ((rel >= 0) & (rel <= span))[None] & (key_idx >= 0)[:, None, :]

    s = jnp.einsum('bgnqhd,bgnkhd->bgnhqk', qb, kb).astype(jnp.float32) * (Dh ** -0.5)
    dist = jnp.abs(pq[..., :, None] - pk[..., None, :]).astype(jnp.float32)
    s = s - slopes[:, None, None] * dist[:, :, :, None]
    s = jnp.where(valid[None, None, :, None], s, -jnp.inf)
    m = jnp.max(s, axis=-1, keepdims=True)
    p = jnp.exp(s - m)
    l = jnp.sum(p, axis=-1)
    o = jnp.einsum('bgnhqk,bgnkhd->bgnqhd', p, vb.astype(jnp.float32))
    o = o / jnp.swapaxes(l, -1, -2)[..., None]
    lse = jnp.swapaxes(m[..., 0] + jnp.log(l), -1, -2)

    def from_sub(a):
        a = a.reshape((B, dilation, Lp) + a.shape[4:])[:, :, :L]
        return jnp.swapaxes(a, 1, 2).reshape((B, S) + a.shape[3:])

    return from_sub(o), from_sub(lse)


def dilated_mixer(h, pos, w_qkv, w_out):
    B, S, _ = h.shape
    qkv = (h @ w_qkv).reshape(B, S, N_DIL_GROUPS, 3, DSA_HEADS, DSA_HEAD_DIM)
    slopes = alibi_slopes(DSA_HEADS)
    outs, lses = [], []
    for g, (window, dil) in enumerate(DIL_PAIRS):
        o, lse = dilated_window_attention(qkv[:, :, g, 0], qkv[:, :, g, 1], qkv[:, :, g, 2],
                                          pos, dil, window // dil, slopes)
        outs.append(o)
        lses.append(lse)
    wts = jax.nn.softmax(jnp.stack(lses), axis=0)
    mixed = jnp.sum(wts[..., None] * jnp.stack(outs), axis=0).astype(h.dtype)
    return mixed.reshape(B, S, DSA_WIDTH) @ w_out


def rope_tables(pos):
    inv = ROPE_THETA ** (-jnp.arange(0, QK_ROPE, 2, dtype=jnp.float32) / QK_ROPE)
    ang = pos.astype(jnp.float32)[..., None] * inv
    return jnp.cos(ang)[:, :, None, :], jnp.sin(ang)[:, :, None, :]


def apply_rope(x, cos, sin):
    xf = x.astype(jnp.float32)
    x1, x2 = xf[..., :QK_ROPE // 2], xf[..., QK_ROPE // 2:]
    return jnp.concatenate([x1 * cos - x2 * sin, x1 * sin + x2 * cos], axis=-1).astype(x.dtype)


def causal_block_attention(q, k, v, scale):
    B, S, H, Dq = q.shape
    Dv = v.shape[-1]
    nb = S // BLOCK
    qb = jnp.moveaxis(q.reshape(B, nb, BLOCK, H, Dq), 1, 0)
    kb = jnp.moveaxis(k.reshape(B, nb, BLOCK, H, Dq), 1, 0)
    vb = jnp.moveaxis(v.reshape(B, nb, BLOCK, H, Dv), 1, 0)
    ar = jnp.arange(BLOCK)

    def one_block(args):
        i, qi = args

        def body(j, carry):
            m, l, acc = carry
            kj = lax.dynamic_index_in_dim(kb, j, axis=0, keepdims=False)
            vj = lax.dynamic_index_in_dim(vb, j, axis=0, keepdims=False)
            s = jnp.einsum('bqhd,bkhd->bhqk', qi, kj).astype(jnp.float32) * scale
            mask = (j * BLOCK + ar)[None, :] <= (i * BLOCK + ar)[:, None]
            s = jnp.where(mask[None, None], s, -jnp.inf)
            m_new = jnp.maximum(m, jnp.max(s, axis=-1))
            corr = jnp.exp(m - m_new)
            p = jnp.exp(s - m_new[..., None])
            l = l * corr + jnp.sum(p, axis=-1)
            acc = acc * corr[..., None] + jnp.einsum('bhqk,bkhd->bhqd', p, vj.astype(jnp.float32))
            return m_new, l, acc

        init = (jnp.full((B, H, BLOCK), -jnp.inf, jnp.float32),
                jnp.zeros((B, H, BLOCK), jnp.float32),
                jnp.zeros((B, H, BLOCK, Dv), jnp.float32))
        _, l, acc = lax.fori_loop(0, i + 1, body, init)
        return jnp.swapaxes(acc / l[..., None], 1, 2)

    out = lax.map(one_block, (jnp.arange(nb), qb))
    return jnp.moveaxis(out, 0, 1).reshape(B, S, H, Dv).astype(q.dtype)


def mla_mixer(h, pos, w_in, q_norm_g, kv_norm_g, w_q_up, w_kv_up, w_out):
    B, S, _ = h.shape
    a = h @ w_in
    c_q = rmsnorm(a[..., :Q_LORA], q_norm_g)
    c_kv = rmsnorm(a[..., Q_LORA:Q_LORA + KV_LORA], kv_norm_g)
    k_rope = a[..., Q_LORA + KV_LORA:][:, :, None, :]
    q = (c_q @ w_q_up).reshape(B, S, MLA_HEADS, QK_NOPE + QK_ROPE)
    kv = (c_kv @ w_kv_up).reshape(B, S, MLA_HEADS, QK_NOPE + V_DIM)
    cos, sin = rope_tables(pos)
    q = jnp.concatenate([q[..., :QK_NOPE], apply_rope(q[..., QK_NOPE:], cos, sin)], axis=-1)
    k_rope = jnp.broadcast_to(apply_rope(k_rope, cos, sin), (B, S, MLA_HEADS, QK_ROPE))
    k = jnp.concatenate([kv[..., :QK_NOPE], k_rope], axis=-1)
    v = kv[..., QK_NOPE:]
    o = causal_block_attention(q, k, v, (QK_NOPE + QK_ROPE) ** -0.5)
    return o.reshape(B, S, MLA_HEADS * V_DIM) @ w_out


def swiglu(h, w_gate, w_up, w_down):
    return (jax.nn.silu(h @ w_gate) * (h @ w_up)) @ w_down


def moe_swiglu(h, w_router, w_gate, w_up, w_down):
    logits = (h @ w_router).astype(jnp.float32)
    top_logits, top_idx = lax.top_k(logits, TOP_K)
    top_w = jax.nn.softmax(top_logits, axis=-1)
    gates = jnp.sum(jax.nn.one_hot(top_idx, N_EXPERTS, dtype=jnp.float32) * top_w[..., None], axis=-2)
    y = jnp.zeros(h.shape, jnp.float32)
    for e in range(N_EXPERTS):
        y = y + gates[..., e:e + 1] * swiglu(h, w_gate[e], w_up[e], w_down[e]).astype(jnp.float32)
    return y.astype(h.dtype)


def setup_inputs(seed: int = 0) -> dict:
    key = jax.random.key(seed)
    ks = iter(jax.random.split(key, 32))
    na = (DEPTH + 1) // 2
    nb = DEPTH // 2

    def nrm(shape, fan_in):
        return jax.random.normal(next(ks), shape, jnp.float32) * (fan_in ** -0.5)

    def gain(shape):
        return 1.0 + 0.02 * jax.random.normal(next(ks), shape, jnp.float32)

    x = jax.random.normal(next(ks), (BATCH, SEQ, D_MODEL), jnp.float32)
    offsets = jax.random.randint(next(ks), (BATCH, 1), 0, 4096, dtype=jnp.int32)
    positions = offsets + jnp.arange(SEQ, dtype=jnp.int32)[None, :]
    return {
        "x": x,
        "positions": positions,
        "mix_norm_g": gain((DEPTH, D_MODEL)),
        "ffn_norm_g": gain((DEPTH, D_MODEL)),
        "a_w_qkv": nrm((na, D_MODEL, DSA_QKV_COLS), D_MODEL),
        "a_w_out": nrm((na, DSA_WIDTH, D_MODEL), DSA_WIDTH),
        "b_w_in": nrm((nb, D_MODEL, MLA_IN_COLS), D_MODEL),
        "b_q_norm_g": gain((nb, Q_LORA)),
        "b_kv_norm_g": gain((nb, KV_LORA)),
        "b_w_q_up": nrm((nb, Q_LORA, MLA_HEADS * (QK_NOPE + QK_ROPE)), Q_LORA),
        "b_w_kv_up": nrm((nb, KV_LORA, MLA_HEADS * (QK_NOPE + V_DIM)), KV_LORA),
        "b_w_out": nrm((nb, MLA_HEADS * V_DIM, D_MODEL), MLA_HEADS * V_DIM),
        "ffn_w_gate": nrm((na, D_MODEL, D_FF_DENSE), D_MODEL),
        "ffn_w_up": nrm((na, D_MODEL, D_FF_DENSE), D_MODEL),
        "ffn_w_down": nrm((na, D_FF_DENSE, D_MODEL), D_FF_DENSE),
        "moe_w_router": nrm((nb, D_MODEL, N_EXPERTS), D_MODEL),
        "moe_w_gate": nrm((nb, N_EXPERTS, D_MODEL, D_FF_EXPERT), D_MODEL),
        "moe_w_up": nrm((nb, N_EXPERTS, D_MODEL, D_FF_EXPERT), D_MODEL),
        "moe_w_down": nrm((nb, N_EXPERTS, D_FF_EXPERT, D_MODEL), D_FF_EXPERT),
        "final_norm_g": gain((D_MODEL,)),
    }


def reference(x, positions, mix_norm_g, ffn_norm_g, a_w_qkv, a_w_out, b_w_in, b_q_norm_g,
              b_kv_norm_g, b_w_q_up, b_w_kv_up, b_w_out, ffn_w_gate, ffn_w_up, ffn_w_down,
              moe_w_router, moe_w_gate, moe_w_up, moe_w_down, final_norm_g):
    for i in range(DEPTH):
        j = i // 2
        h = rmsnorm(x, mix_norm_g[i])
        if i % 2 == 0:
            x = x + dilated_mixer(h, positions, a_w_qkv[j], a_w_out[j])
        else:
            x = x + mla_mixer(h, positions, b_w_in[j], b_q_norm_g[j], b_kv_norm_g[j],
                              b_w_q_up[j], b_w_kv_up[j], b_w_out[j])
        h = rmsnorm(x, ffn_norm_g[i])
        if i % 2 == 0:
            x = x + swiglu(h, ffn_w_gate[j], ffn_w_up[j], ffn_w_down[j])
        else:
            x = x + moe_swiglu(h, moe_w_router[j], moe_w_gate[j], moe_w_up[j], moe_w_down[j])
    return rmsnorm(x, final_norm_g)
```

```python
import functools

import jax
import jax.numpy as jnp
from jax import lax
from jax.experimental import pallas as pl
from jax.experimental.pallas import tpu as pltpu

F32 = jnp.float32
BF16 = jnp.bfloat16

RMS_EPS = 1e-6
BLOCK = 128
LANES = 128
DIL_PAIRS = ((128, 1), (512, 4), (2048, 16))
DSA_HEADS = 8
DSA_HEAD_DIM = 64
DSA_WIDTH = DSA_HEADS * DSA_HEAD_DIM
MLA_HEADS = 16
QK_NOPE = 64
QK_ROPE = 32
V_DIM = 64
Q_LORA = 384
KV_LORA = 256
ROPE_THETA = 10000.0
N_EXPERTS = 8
MASK_VALUE = -1e30
VMEM_LIMIT = 56 << 20


def _params(semantics, vmem=VMEM_LIMIT):
    return pltpu.CompilerParams(dimension_semantics=semantics, vmem_limit_bytes=vmem)


def _rms(x, g):
    ms = jnp.mean(x * x, axis=-1, keepdims=True)
    return x * lax.rsqrt(ms + RMS_EPS) * g


def _norm_matmul_kernel(x_ref, g_ref, w_ref, o_ref):
    h = _rms(x_ref[...], g_ref[...]).astype(BF16)
    o_ref[...] = jnp.dot(h, w_ref[...], preferred_element_type=F32).astype(o_ref.dtype)


def _norm_matmul(x, g, w, *, tm, tn, out_dtype):
    m, d = x.shape
    n = w.shape[1]
    return pl.pallas_call(
        _norm_matmul_kernel,
        grid=(n // tn, m // tm),
        in_specs=[pl.BlockSpec((tm, d), lambda j, i: (i, 0)),
                  pl.BlockSpec((1, d), lambda j, i: (0, 0)),
                  pl.BlockSpec((d, tn), lambda j, i: (0, j))],
        out_specs=pl.BlockSpec((tm, tn), lambda j, i: (i, j)),
        out_shape=jax.ShapeDtypeStruct((m, n), out_dtype),
        compiler_params=_params(("arbitrary", "arbitrary")),
        name="norm_matmul",
    )(x, g.reshape(1, d), w)


def _matmul_residual_kernel(x_ref, a_ref, w_ref, o_ref):
    o_ref[...] = x_ref[...] + jnp.dot(a_ref[...], w_ref[...], preferred_element_type=F32)


def _matmul_residual(x, a, w, *, tm):
    m, d = x.shape
    k = a.shape[1]
    return pl.pallas_call(
        _matmul_residual_kernel,
        grid=(m // tm,),
        in_specs=[pl.BlockSpec((tm, d), lambda i: (i, 0)),
                  pl.BlockSpec((tm, k), lambda i: (i, 0)),
                  pl.BlockSpec((k, d), lambda i: (0, 0))],
        out_specs=pl.BlockSpec((tm, d), lambda i: (i, 0)),
        out_shape=jax.ShapeDtypeStruct((m, d), F32),
        compiler_params=_params(("arbitrary",)),
        name="matmul_residual",
    )(x, a, w)


def _dilated_kernel(*refs, qb, span, has_prev, emit_lse):
    q_ref, k_ref, v_ref, pcol_ref, prow_ref = refs[:5]
    pos = 5
    if has_prev:
        oprev_ref, lprev_ref = refs[pos:pos + 2]
        pos += 2
    o_ref = refs[pos]
    pos += 1
    if emit_lse:
        lse_ref = refs[pos]
        pos += 1
    kbuf, vbuf, pbuf = refs[pos:pos + 3]

    n = pl.program_id(2)

    @pl.when(n == 0)
    def _():
        kbuf[0:BLOCK, :] = jnp.zeros((BLOCK, DSA_WIDTH), BF16)
        vbuf[0:BLOCK, :] = jnp.zeros((BLOCK, DSA_WIDTH), BF16)
        pbuf[:, 0:BLOCK] = jnp.zeros((1, BLOCK), jnp.int32)

    kbuf[BLOCK:, :] = k_ref[...]
    vbuf[BLOCK:, :] = v_ref[...]
    pbuf[:, BLOCK:] = prow_ref[...]

    lane = lax.broadcasted_iota(jnp.int32, (BLOCK, LANES), 1)
    low = lane < DSA_HEAD_DIM
    row = lax.broadcasted_iota(jnp.int32, (BLOCK, 2 * BLOCK), 0)
    col = lax.broadcasted_iota(jnp.int32, (BLOCK, 2 * BLOCK), 1)
    rel = row + BLOCK - col
    band = (rel >= 0) & (rel <= span)
    first_valid = band & ((col >= BLOCK) | (n > 0))
    scale = DSA_HEAD_DIM ** -0.5

    for i in range(qb):
        rows = slice(i * BLOCK, (i + 1) * BLOCK)
        keys = slice(i * BLOCK, (i + 2) * BLOCK)
        kk = kbuf[keys, :]
        vv = vbuf[keys, :]
        pk = pbuf[:, keys]
        pq = pcol_ref[rows, :]
        dist = jnp.abs(jnp.concatenate([pq, pq], axis=1) - pk).astype(F32)
        valid = first_valid if i == 0 else band
        q = q_ref[rows, :]
        if emit_lse:
            lse_tile = jnp.zeros((BLOCK, LANES), F32)
        if has_prev:
            lprev = lprev_ref[rows, :]
        for p in range(DSA_HEADS // 2):
            cols = slice(p * LANES, (p + 1) * LANES)
            q2, k2, v2 = q[:, cols], kk[:, cols], vv[:, cols]
            pvs, c_new, c_prev = [], [], []
            for half in range(2):
                h = 2 * p + half
                slope = 2.0 ** (-8.0 * (h + 1) / DSA_HEADS)
                qm = jnp.where(low if half == 0 else ~low, q2, jnp.zeros_like(q2))
                s = lax.dot_general(qm, k2, (((1,), (1,)), ((), ())), preferred_element_type=F32)
                s = s * scale - slope * dist
                s = jnp.where(valid, s, MASK_VALUE)
                m = jnp.max(s, axis=-1, keepdims=True)
                e = jnp.exp(s - m)
                l = jnp.sum(e, axis=-1, keepdims=True)
                pvs.append(jnp.dot(e.astype(BF16), v2, preferred_element_type=F32))
                lse = m + jnp.log(l)
                if has_prev:
                    lp = lprev[:, h:h + 1]
                    mx = jnp.maximum(lp, lse)
                    wa = jnp.exp(lp - mx)
                    wb = jnp.exp(lse - mx)
                    den = wa + wb
                    c_prev.append(wa / den)
                    c_new.append(wb / (den * l))
                    lse = mx + jnp.log(den)
                else:
                    c_new.append(1.0 / l)
                if emit_lse:
                    lse_tile = jnp.where(lane == h, lse, lse_tile)
            out = jnp.where(low, pvs[0] * c_new[0], pvs[1] * c_new[1])
            if has_prev:
                out = out + oprev_ref[rows, cols].astype(F32) * jnp.where(low, c_prev[0], c_prev[1])
            o_ref[rows, cols] = out.astype(o_ref.dtype)
        if emit_lse:
            lse_ref[rows, :] = lse_tile

    tail = slice(qb * BLOCK, (qb + 1) * BLOCK)
    kbuf[0:BLOCK, :] = kbuf[tail, :]
    vbuf[0:BLOCK, :] = vbuf[tail, :]
    pbuf[:, 0:BLOCK] = pbuf[:, tail]


def _dilated_group(qkv, positions, prev, *, group, emit_lse, qb):
    b, s, cols = qkv.shape
    window, dil = DIL_PAIRS[group]
    span = window // dil
    sub = s // dil
    tq = qb * BLOCK
    nb = sub // tq
    colblocks = cols // DSA_WIDTH
    qkv_v = qkv.reshape(b, sub, dil * cols)
    pcol = jnp.broadcast_to(positions[:, :, None], (b, s, LANES)).reshape(b, sub, dil * LANES)
    prow = positions.reshape(b, sub, dil).transpose(0, 2, 1).reshape(b, dil, 1, sub)

    def qkv_spec(which):
        return pl.BlockSpec((None, tq, DSA_WIDTH),
                            lambda bi, r, n: (bi, n, r * colblocks + group * 3 + which))

    in_specs = [qkv_spec(0), qkv_spec(1), qkv_spec(2),
                pl.BlockSpec((None, tq, LANES), lambda bi, r, n: (bi, n, r)),
                pl.BlockSpec((None, None, 1, tq), lambda bi, r, n: (bi, r, 0, n))]
    args = [qkv_v, qkv_v, qkv_v, pcol, prow]
    if prev is not None:
        o_prev, l_prev = prev
        in_specs += [pl.BlockSpec((None, tq, DSA_WIDTH), lambda bi, r, n: (bi, n, r)),
                     pl.BlockSpec((None, tq, LANES), lambda bi, r, n: (bi, n, r))]
        args += [o_prev.reshape(b, sub, dil * DSA_WIDTH), l_prev.reshape(b, sub, dil * LANES)]
    out_specs = [pl.BlockSpec((None, tq, DSA_WIDTH), lambda bi, r, n: (bi, n, r))]
    out_shape = [jax.ShapeDtypeStruct((b, sub, dil * DSA_WIDTH), BF16)]
    if emit_lse:
        out_specs.append(pl.BlockSpec((None, tq, LANES), lambda bi, r, n: (bi, n, r)))
        out_shape.append(jax.ShapeDtypeStruct((b, sub, dil * LANES), F32))
    outs = pl.pallas_call(
        functools.partial(_dilated_kernel, qb=qb, span=span, has_prev=prev is not None,
                          emit_lse=emit_lse),
        grid=(b, dil, nb),
        in_specs=in_specs,
        out_specs=out_specs,
        out_shape=out_shape,
        scratch_shapes=[pltpu.VMEM((tq + BLOCK, DSA_WIDTH), BF16),
                        pltpu.VMEM((tq + BLOCK, DSA_WIDTH), BF16),
                        pltpu.VMEM((1, tq + BLOCK), jnp.int32)],
        compiler_params=_params(("arbitrary", "arbitrary", "arbitrary")),
        name=f"dilated_group{group}",
    )(*args)
    o = outs[0].reshape(b, s, DSA_WIDTH)
    if emit_lse:
        return o, outs[1].reshape(b, s, LANES)
    return o


def _swiglu_kernel(x_ref, g_ref, wg_ref, wu_ref, wd_ref, o_ref, h_scr, acc_scr):
    f = pl.program_id(1)

    @pl.when(f == 0)
    def _():
        x = x_ref[...]
        h_scr[...] = _rms(x, g_ref[...]).astype(BF16)
        acc_scr[...] = x

    h = h_scr[...]
    gate = jnp.dot(h, wg_ref[...], preferred_element_type=F32)
    up = jnp.dot(h, wu_ref[...], preferred_element_type=F32)
    act = (gate * jax.nn.sigmoid(gate) * up).astype(BF16)
    acc_scr[...] += jnp.dot(act, wd_ref[...], preferred_element_type=F32)

    @pl.when(f == pl.num_programs(1) - 1)
    def _():
        o_ref[...] = acc_scr[...]


def _swiglu(x, g, wg, wu, wd, *, tm, tf):
    m, d = x.shape
    ff = wg.shape[1]
    return pl.pallas_call(
        _swiglu_kernel,
        grid=(m // tm, ff // tf),
        in_specs=[pl.BlockSpec((tm, d), lambda i, f: (i, 0)),
                  pl.BlockSpec((1, d), lambda i, f: (0, 0)),
                  pl.BlockSpec((d, tf), lambda i, f: (0, f)),
                  pl.BlockSpec((d, tf), lambda i, f: (0, f)),
                  pl.BlockSpec((tf, d), lambda i, f: (f, 0))],
        out_specs=pl.BlockSpec((tm, d), lambda i, f: (i, 0)),
        out_shape=jax.ShapeDtypeStruct((m, d), F32),
        scratch_shapes=[pltpu.VMEM((tm, d), BF16), pltpu.VMEM((tm, d), F32)],
        compiler_params=_params(("arbitrary", "arbitrary")),
        name="swiglu",
    )(x, g.reshape(1, d), wg, wu, wd)


def _rope_lanes(x, c, sa, sb):
    return x * c + pltpu.roll(x, LANES - QK_ROPE // 2, 1) * sa + pltpu.roll(x, QK_ROPE // 2, 1) * sb


def _mla_project_kernel(x_ref, g_ref, win_ref, gq_ref, gkv_ref, wq_ref, wk_ref, wkr_ref, wv_ref,
                        c_ref, sa_ref, sb_ref, q_ref, k_ref, v_ref, *, scale):
    h = _rms(x_ref[...], g_ref[...]).astype(BF16)
    a = jnp.dot(h, win_ref[...], preferred_element_type=F32)
    c_q = _rms(a[:, :Q_LORA], gq_ref[...]).astype(BF16)
    c_kv = _rms(a[:, Q_LORA:Q_LORA + KV_LORA], gkv_ref[...]).astype(BF16)
    k_rope = a[:, Q_LORA + KV_LORA:].astype(BF16)
    q = jnp.dot(c_q, wq_ref[...], preferred_element_type=F32)
    k = (jnp.dot(c_kv, wk_ref[...], preferred_element_type=F32)
         + jnp.dot(k_rope, wkr_ref[...], preferred_element_type=F32))
    v = jnp.dot(c_kv, wv_ref[...], preferred_element_type=F32)
    c, sa, sb = c_ref[...], sa_ref[...], sb_ref[...]
    for p in range(MLA_HEADS // 2):
        qs, ks = [], []
        for half in range(2):
            cols = slice((2 * p + half) * LANES, (2 * p + half + 1) * LANES)
            qs.append((_rope_lanes(q[:, cols], c, sa, sb) * scale).astype(BF16))
            ks.append(_rope_lanes(k[:, cols], c, sa, sb).astype(BF16))
        q_ref[p] = jnp.concatenate(qs, axis=1)
        k_ref[p] = jnp.concatenate(ks, axis=1)
        v_ref[p] = v[:, p * LANES:(p + 1) * LANES].astype(BF16)


def _mla_project(x, g, w_in, gq, gkv, wq_pad, wk_pad, wkr, wv, tables, *, tm):
    b, s, d = x.shape
    c, sa, sb = tables
    pairs = MLA_HEADS // 2
    scale = (QK_NOPE + QK_ROPE) ** -0.5

    def full(arr):
        return pl.BlockSpec(arr.shape, lambda bi, i: (0,) * arr.ndim)

    tab = pl.BlockSpec((None, tm, LANES), lambda bi, i: (bi, i, 0))
    ops = [g.reshape(1, d), w_in, gq.reshape(1, -1), gkv.reshape(1, -1), wq_pad, wk_pad, wkr, wv]
    return pl.pallas_call(
        functools.partial(_mla_project_kernel, scale=scale),
        grid=(b, s // tm),
        in_specs=[pl.BlockSpec((None, tm, d), lambda bi, i: (bi, i, 0))] + [full(o) for o in ops]
                 + [tab, tab, tab],
        out_specs=[pl.BlockSpec((None, pairs, tm, 2 * LANES), lambda bi, i: (bi, 0, i, 0)),
                   pl.BlockSpec((None, pairs, tm, 2 * LANES), lambda bi, i: (bi, 0, i, 0)),
                   pl.BlockSpec((None, pairs, tm, LANES), lambda bi, i: (bi, 0, i, 0))],
        out_shape=[jax.ShapeDtypeStruct((b, pairs, s, 2 * LANES), BF16),
                   jax.ShapeDtypeStruct((b, pairs, s, 2 * LANES), BF16),
                   jax.ShapeDtypeStruct((b, pairs, s, LANES), BF16)],
        compiler_params=_params(("arbitrary", "arbitrary")),
        name="mla_project",
    )(x, *ops, c, sa, sb)


def _flash_kernel(q_ref, k_ref, v_ref, o_ref, m_scr, l_scr, acc_scr, *, t):
    qi = pl.program_id(2)
    m_scr[...] = jnp.full(m_scr.shape, MASK_VALUE, F32)
    l_scr[...] = jnp.zeros(l_scr.shape, F32)
    acc_scr[...] = jnp.zeros(acc_scr.shape, F32)
    q = q_ref[...]
    low = lax.broadcasted_iota(jnp.int32, (t, LANES), 1) < V_DIM

    def tile(kj, masked):
        start = pl.multiple_of(kj * t, t)
        ks = k_ref[pl.ds(start, t), :]
        vs = v_ref[pl.ds(start, t), :]
        corr, pv = [], []
        for half in range(2):
            cols = slice(half * LANES, (half + 1) * LANES)
            s = lax.dot_general(q[:, cols], ks[:, cols], (((1,), (1,)), ((), ())),
                                preferred_element_type=F32)
            if masked:
                row = lax.broadcasted_iota(jnp.int32, (t, t), 0)
                col = lax.broadcasted_iota(jnp.int32, (t, t), 1)
                s = jnp.where(row >= col, s, MASK_VALUE)
            m_prev = m_scr[half]
            m_new = jnp.maximum(m_prev, jnp.max(s, axis=-1, keepdims=True))
            c = jnp.exp(m_prev - m_new)
            e = jnp.exp(s - jnp.tile(m_new, (1, t // LANES)))
            l_scr[half] = c * l_scr[half] + jnp.sum(e, axis=-1, keepdims=True)
            m_scr[half] = m_new
            corr.append(c)
            pv.append(jnp.dot(e.astype(BF16), vs, preferred_element_type=F32))
        acc_scr[...] = (acc_scr[...] * jnp.where(low, corr[0], corr[1])
                        + jnp.where(low, pv[0], pv[1]))

    def body(kj, carry):
        tile(kj, False)
        return carry

    lax.fori_loop(0, qi, body, 0)
    tile(qi, True)
    o_ref[...] = (acc_scr[...] / jnp.where(low, l_scr[0], l_scr[1])).astype(o_ref.dtype)


def _flash_attention(q, k, v, *, t):
    b, pairs, s, _ = q.shape
    return pl.pallas_call(
        functools.partial(_flash_kernel, t=t),
        grid=(b, pairs, s // t),
        in_specs=[pl.BlockSpec((None, None, t, 2 * LANES), lambda bi, p, i: (bi, p, i, 0)),
                  pl.BlockSpec((None, None, s, 2 * LANES), lambda bi, p, i: (bi, p, 0, 0)),
                  pl.BlockSpec((None, None, s, LANES), lambda bi, p, i: (bi, p, 0, 0))],
        out_specs=pl.BlockSpec((None, t, LANES), lambda bi, p, i: (bi, i, p)),
        out_shape=jax.ShapeDtypeStruct((b, s, pairs * LANES), BF16),
        scratch_shapes=[pltpu.VMEM((2, t, LANES), F32), pltpu.VMEM((2, t, LANES), F32),
                        pltpu.VMEM((t, LANES), F32)],
        compiler_params=_params(("arbitrary", "arbitrary", "arbitrary")),
        name="flash_attention",
    )(q, k, v)


def _route_kernel(x_ref, g_ref, wr_ref, meta_ref, cnt_ref, carry):
    i = pl.program_id(0)
    t = x_ref.shape[0]

    @pl.when(i == 0)
    def _():
        carry[...] = jnp.zeros(carry.shape, F32)

    h = _rms(x_ref[...], g_ref[...])
    logits = jnp.dot(h, wr_ref[...], preferred_element_type=F32, precision=lax.Precision.HIGHEST)
    lane = lax.broadcasted_iota(jnp.int32, (t, LANES), 1)
    lg = jnp.where(lane < N_EXPERTS, logits, -jnp.inf)
    m1 = jnp.max(lg, axis=-1, keepdims=True)
    i1 = jnp.min(jnp.where(lg == m1, lane, LANES), axis=-1, keepdims=True)
    lg2 = jnp.where(lane == i1, -jnp.inf, lg)
    m2 = jnp.max(lg2, axis=-1, keepdims=True)
    i2 = jnp.min(jnp.where(lg2 == m2, lane, LANES), axis=-1, keepdims=True)
    e2 = jnp.exp(m2 - m1)
    w1 = 1.0 / (1.0 + e2)
    w2 = e2 / (1.0 + e2)
    hot1 = lane == i1
    hot2 = lane == i2
    onehot = (hot1 | hot2).astype(BF16)
    r = lax.broadcasted_iota(jnp.int32, (t, t), 0)
    c = lax.broadcasted_iota(jnp.int32, (t, t), 1)
    lower = (c < r).astype(BF16)
    before = jnp.dot(lower, onehot, preferred_element_type=F32) + carry[...]
    rank1 = jnp.sum(jnp.where(hot1, before, 0.0), axis=-1, keepdims=True)
    rank2 = jnp.sum(jnp.where(hot2, before, 0.0), axis=-1, keepdims=True)
    carry[...] += jnp.sum(onehot.astype(F32), axis=0, keepdims=True)
    meta = jnp.zeros((t, LANES), F32)
    for k, val in enumerate((i1.astype(F32), i2.astype(F32), rank1, rank2, w1, w2)):
        meta = jnp.where(lane == k, val, meta)
    meta_ref[...] = meta
    cnt_ref[...] = carry[...]


def _route(x, g, w_router, *, tm):
    m, d = x.shape
    wr = jnp.zeros((d, LANES), F32).at[:, :N_EXPERTS].set(w_router)
    return pl.pallas_call(
        _route_kernel,
        grid=(m // tm,),
        in_specs=[pl.BlockSpec((tm, d), lambda i: (i, 0)),
                  pl.BlockSpec((1, d), lambda i: (0, 0)),
                  pl.BlockSpec((d, LANES), lambda i: (0, 0))],
        out_specs=[pl.BlockSpec((tm, LANES), lambda i: (i, 0)),
                   pl.BlockSpec((1, LANES), lambda i: (0, 0))],
        out_shape=[jax.ShapeDtypeStruct((m, LANES), F32), jax.ShapeDtypeStruct((1, LANES), F32)],
        scratch_shapes=[pltpu.VMEM((1, LANES), F32)],
        compiler_params=_params(("arbitrary",)),
        name="route",
    )(x, g.reshape(1, d), wr)


def _dispatch_kernel(dest_ref, x_ref, g_ref, init_ref, xs_ref, h_scr, sem):
    del init_ref
    t = x_ref.shape[0]
    h_scr[...] = _rms(x_ref[...], g_ref[...])

    def row_copy(j, slot):
        return pltpu.make_async_copy(h_scr.at[pl.ds(j, 1)],
                                     xs_ref.at[pl.ds(dest_ref[2 * j + slot], 1)], sem)

    def start(j, carry):
        row_copy(j, 0).start()
        row_copy(j, 1).start()
        return carry

    def wait(j, carry):
        row_copy(j, 0).wait()
        row_copy(j, 1).wait()
        return carry

    lax.fori_loop(0, t, start, 0)
    lax.fori_loop(0, t, wait, 0)


def _dispatch(x, g, dest, rows, *, tm):
    m, d = x.shape
    init = jnp.zeros((rows, d), F32)
    return pl.pallas_call(
        _dispatch_kernel,
        grid=(m // tm,),
        in_specs=[pl.BlockSpec((2 * tm,), lambda i: (i,), memory_space=pltpu.SMEM),
                  pl.BlockSpec((tm, d), lambda i: (i, 0)),
                  pl.BlockSpec((1, d), lambda i: (0, 0)),
                  pl.BlockSpec(memory_space=pl.ANY)],
        out_specs=pl.BlockSpec(memory_space=pl.ANY),
        out_shape=jax.ShapeDtypeStruct((rows, d), F32),
        scratch_shapes=[pltpu.VMEM((tm, d), F32), pltpu.SemaphoreType.DMA(())],
        input_output_aliases={3: 0},
        compiler_params=_params(("arbitrary",)),
        name="dispatch",
    )(dest, x, g.reshape(1, d), init)


def _grouped_kernel(te_ref, tv_ref, xs_ref, wg_ref, wu_ref, wd_ref, y_ref, acc_scr):
    i = pl.program_id(0)
    f = pl.program_id(1)
    last = pl.num_programs(1) - 1
    valid = tv_ref[i] > 0

    @pl.when(valid)
    def _():
        h = xs_ref[...].astype(BF16)
        gate = jnp.dot(h, wg_ref[...], preferred_element_type=F32)
        up = jnp.dot(h, wu_ref[...], preferred_element_type=F32)
        act = (gate * jax.nn.sigmoid(gate) * up).astype(BF16)
        part = jnp.dot(act, wd_ref[...], preferred_element_type=F32)

        @pl.when(f == 0)
        def _():
            acc_scr[...] = part

        @pl.when(f > 0)
        def _():
            acc_scr[...] += part

        @pl.when(f == last)
        def _():
            y_ref[...] = acc_scr[...]

    @pl.when(jnp.logical_not(valid) & (f == last))
    def _():
        y_ref[...] = jnp.zeros(y_ref.shape, F32)


def _grouped_swiglu(xs, tile_expert, tile_valid, wg, wu, wd, *, tm, tf):
    rows, d = xs.shape
    ff = wg.shape[2]
    grid_spec = pltpu.PrefetchScalarGridSpec(
        num_scalar_prefetch=2,
        grid=(rows // tm, ff // tf),
        in_specs=[pl.BlockSpec((tm, d), lambda i, f, te, tv: (i, 0)),
                  pl.BlockSpec((None, d, tf), lambda i, f, te, tv: (te[i], 0, f)),
                  pl.BlockSpec((None, d, tf), lambda i, f, te, tv: (te[i], 0, f)),
                  pl.BlockSpec((None, tf, d), lambda i, f, te, tv: (te[i], f, 0))],
        out_specs=pl.BlockSpec((tm, d), lambda i, f, te, tv: (i, 0)),
        scratch_shapes=[pltpu.VMEM((tm, d), F32)],
    )
    return pl.pallas_call(
        _grouped_kernel,
        grid_spec=grid_spec,
        out_shape=jax.ShapeDtypeStruct((rows, d), F32),
        compiler_params=_params(("arbitrary", "arbitrary")),
        name="grouped_swiglu",
    )(tile_expert, tile_valid, xs, wg, wu, wd)


def _combine_kernel(dest_ref, x_ref, meta_ref, g_ref, y_ref, o_ref, y0_scr, y1_scr, sem):
    t = x_ref.shape[0]

    def row_copy(j, slot):
        dst = y0_scr if slot == 0 else y1_scr
        return pltpu.make_async_copy(y_ref.at[pl.ds(dest_ref[2 * j + slot], 1)],
                                     dst.at[pl.ds(j, 1)], sem)

    def start(j, carry):
        row_copy(j, 0).start()
        row_copy(j, 1).start()
        return carry

    def wait(j, carry):
        row_copy(j, 0).wait()
        row_copy(j, 1).wait()
        return carry

    lax.fori_loop(0, t, start, 0)
    lax.fori_loop(0, t, wait, 0)
    meta = meta_ref[...]
    x = x_ref[...] + (meta[:, 4:5] * y0_scr[...] + meta[:, 5:6] * y1_scr[...])
    o_ref[...] = _rms(x, g_ref[...])


def _combine(x, meta, dest, y, g, *, tm):
    m, d = x.shape
    return pl.pallas_call(
        _combine_kernel,
        grid=(m // tm,),
        in_specs=[pl.BlockSpec((2 * tm,), lambda i: (i,), memory_space=pltpu.SMEM),
                  pl.BlockSpec((tm, d), lambda i: (i, 0)),
                  pl.BlockSpec((tm, LANES), lambda i: (i, 0)),
                  pl.BlockSpec((1, d), lambda i: (0, 0)),
                  pl.BlockSpec(memory_space=pl.ANY)],
        out_specs=pl.BlockSpec((tm, d), lambda i: (i, 0)),
        out_shape=jax.ShapeDtypeStruct((m, d), F32),
        scratch_shapes=[pltpu.VMEM((tm, d), F32), pltpu.VMEM((tm, d), F32),
                        pltpu.SemaphoreType.DMA(())],
        compiler_params=_params(("arbitrary",)),
        name="combine",
    )(dest, x, meta, g.reshape(1, d), y)


def _moe_layer(x, g, w_router, wg, wu, wd, final_g, *, tm, tf):
    m, d = x.shape
    meta, counts = _route(x, g, w_router, tm=512)
    cnt = counts[0, :N_EXPERTS].astype(jnp.int32)
    tiles = (cnt + tm - 1) // tm
    tile_end = jnp.cumsum(tiles)
    offset = (tile_end - tiles) * tm
    ids = meta[:, 0:2].astype(jnp.int32)
    expert_ids = jnp.arange(N_EXPERTS, dtype=jnp.int32)
    base = jnp.sum(jnp.where(ids[:, :, None] == expert_ids, offset, 0), axis=-1)
    dest = (base + meta[:, 2:4].astype(jnp.int32)).reshape(2 * m)
    n_tiles = (2 * m) // tm + N_EXPERTS
    tile_id = jnp.arange(n_tiles, dtype=jnp.int32)
    tile_expert = jnp.sum(tile_id[:, None] >= tile_end[None, :], axis=1).astype(jnp.int32)
    tile_valid = (tile_expert < N_EXPERTS).astype(jnp.int32)
    tile_expert = jnp.minimum(tile_expert, N_EXPERTS - 1)
    xs = _dispatch(x, g, dest, n_tiles * tm, tm=256)
    y = _grouped_swiglu(xs, tile_expert, tile_valid, wg, wu, wd, tm=tm, tf=tf)
    return _combine(x, meta, dest, y, final_g, tm=256)


def _mla_weights(w_q_up, w_kv_up):
    hd = QK_NOPE + QK_ROPE
    wq = w_q_up.reshape(Q_LORA, MLA_HEADS, hd)
    wq_pad = jnp.pad(wq, ((0, 0), (0, 0), (0, LANES - hd))).reshape(Q_LORA, MLA_HEADS * LANES)
    wkv = w_kv_up.reshape(KV_LORA, MLA_HEADS, QK_NOPE + V_DIM)
    wk_pad = jnp.pad(wkv[:, :, :QK_NOPE], ((0, 0), (0, 0), (0, LANES - QK_NOPE)))
    wk_pad = wk_pad.reshape(KV_LORA, MLA_HEADS * LANES)
    wv = wkv[:, :, QK_NOPE:].reshape(KV_LORA, MLA_HEADS * V_DIM)
    place = jnp.zeros((QK_ROPE, LANES), F32).at[jnp.arange(QK_ROPE), QK_NOPE + jnp.arange(QK_ROPE)].set(1.0)
    wkr = jnp.tile(place, (1, MLA_HEADS))
    return wq_pad.astype(BF16), wk_pad.astype(BF16), wkr.astype(BF16), wv.astype(BF16)


def _rope_lane_tables(positions):
    half = QK_ROPE // 2
    inv = ROPE_THETA ** (-jnp.arange(0, QK_ROPE, 2, dtype=F32) / QK_ROPE)
    ang = positions.astype(F32)[..., None] * inv
    cos, sin = jnp.cos(ang), jnp.sin(ang)
    shape = positions.shape
    ones = jnp.ones(shape + (QK_NOPE,), F32)
    zeros_n = jnp.zeros(shape + (QK_NOPE,), F32)
    zeros_h = jnp.zeros(shape + (half,), F32)
    pad = jnp.zeros(shape + (LANES - QK_NOPE - QK_ROPE,), F32)
    c = jnp.concatenate([ones, cos, cos, pad], axis=-1)
    sa = jnp.concatenate([zeros_n, -sin, zeros_h, pad], axis=-1)
    sb = jnp.concatenate([zeros_n, zeros_h, sin, pad], axis=-1)
    return c, sa, sb


def kernel(x, positions, mix_norm_g, ffn_norm_g, a_w_qkv, a_w_out, b_w_in, b_q_norm_g, b_kv_norm_g,
           b_w_q_up, b_w_kv_up, b_w_out, ffn_w_gate, ffn_w_up, ffn_w_down, moe_w_router, moe_w_gate,
           moe_w_up, moe_w_down, final_norm_g):
    b, s, d = x.shape
    m = b * s
    xf = x.reshape(m, d)

    qkv = _norm_matmul(xf, mix_norm_g[0], a_w_qkv[0].astype(BF16), tm=1024, tn=1536, out_dtype=BF16)
    qkv = qkv.reshape(b, s, -1)
    state = _dilated_group(qkv, positions, None, group=0, emit_lse=True, qb=4)
    state = _dilated_group(qkv, positions, state, group=1, emit_lse=True, qb=4)
    mixed = _dilated_group(qkv, positions, state, group=2, emit_lse=False, qb=4)
    x1 = _matmul_residual(xf, mixed.reshape(m, DSA_WIDTH), a_w_out[0].astype(BF16), tm=1024)
    x2 = _swiglu(x1, ffn_norm_g[0], ffn_w_gate[0].astype(BF16), ffn_w_up[0].astype(BF16),
                 ffn_w_down[0].astype(BF16), tm=1024, tf=256)

    wq_pad, wk_pad, wkr, wv = _mla_weights(b_w_q_up[0], b_w_kv_up[0])
    q, k, v = _mla_project(x2.reshape(b, s, d), mix_norm_g[1], b_w_in[0].astype(BF16), b_q_norm_g[0],
                           b_kv_norm_g[0], wq_pad, wk_pad, wkr, wv, _rope_lane_tables(positions), tm=512)
    attn = _flash_attention(q, k, v, t=512)
    x3 = _matmul_residual(x2, attn.reshape(m, MLA_HEADS * V_DIM), b_w_out[0].astype(BF16), tm=1024)
    out = _moe_layer(x3, ffn_norm_g[1], moe_w_router[0], moe_w_gate[0].astype(BF16),
                     moe_w_up[0].astype(BF16), moe_w_down[0].astype(BF16), final_norm_g,
                     tm=512, tf=512)
    return out.reshape(b, s, d)
```

```python
import functools

import jax
import jax.numpy as jnp
from jax import lax
from jax.experimental import pallas as pl
from jax.experimental.pallas import tpu as pltpu

F32 = jnp.float32
BF16 = jnp.bfloat16

RMS_EPS = 1e-6
BLOCK = 128
LANES = 128
DIL_PAIRS = ((128, 1), (512, 4), (2048, 16))
DSA_HEADS = 8
DSA_HEAD_DIM = 64
DSA_WIDTH = DSA_HEADS * DSA_HEAD_DIM
MLA_HEADS = 16
QK_NOPE = 64
QK_ROPE = 32
V_DIM = 64
Q_LORA = 384
KV_LORA = 256
ROPE_THETA = 10000.0
N_EXPERTS = 8
MASK_VALUE = -1e30
LOG2_E = 1.4426950408889634
VMEM_LIMIT = 56 << 20


def _params(semantics, vmem=VMEM_LIMIT):
    return pltpu.CompilerParams(dimension_semantics=semantics, vmem_limit_bytes=vmem)


def _rms(x, g):
    ms = jnp.mean(x * x, axis=-1, keepdims=True)
    return x * lax.rsqrt(ms + RMS_EPS) * g


def _qkv_kernel(x_ref, g_ref, perm_ref, w_ref, o_ref, *, dil):
    tm = x_ref.shape[0]
    n = tm // dil
    width = w_ref.shape[1]
    h = _rms(x_ref[...], g_ref[...]).astype(BF16)
    if dil > 1:
        h = jnp.dot(perm_ref[...], h, preferred_element_type=F32).astype(BF16)
    res = jnp.dot(h, w_ref[...], preferred_element_type=F32)
    for r in range(dil):
        o_ref[:, r * width:(r + 1) * width] = res[r * n:(r + 1) * n, :].astype(o_ref.dtype)


def _qkv_project(x, g, w, *, dil, tm):
    m, d = x.shape
    width = w.shape[1]
    n = tm // dil
    out_row = jnp.arange(tm, dtype=jnp.int32)
    source = (out_row % n) * dil + out_row // n
    perm = (source[:, None] == out_row[None, :]).astype(BF16)
    return pl.pallas_call(
        functools.partial(_qkv_kernel, dil=dil),
        grid=(m // tm,),
        in_specs=[pl.BlockSpec((tm, d), lambda i: (i, 0)),
                  pl.BlockSpec((1, d), lambda i: (0, 0)),
                  pl.BlockSpec((tm, tm), lambda i: (0, 0)),
                  pl.BlockSpec((d, width), lambda i: (0, 0))],
        out_specs=pl.BlockSpec((n, dil * width), lambda i: (i, 0)),
        out_shape=jax.ShapeDtypeStruct((m // dil, dil * width), BF16),
        compiler_params=_params(("arbitrary",)),
        name=f"qkv_project_d{dil}",
    )(x, g.reshape(1, d), perm, w)


def _matmul_residual_kernel(x_ref, a_ref, w_ref, o_ref):
    o_ref[...] = x_ref[...] + jnp.dot(a_ref[...], w_ref[...], preferred_element_type=F32)


def _matmul_residual(x, a, w, *, tm):
    m, d = x.shape
    k = a.shape[1]
    return pl.pallas_call(
        _matmul_residual_kernel,
        grid=(m // tm,),
        in_specs=[pl.BlockSpec((tm, d), lambda i: (i, 0)),
                  pl.BlockSpec((tm, k), lambda i: (i, 0)),
                  pl.BlockSpec((k, d), lambda i: (0, 0))],
        out_specs=pl.BlockSpec((tm, d), lambda i: (i, 0)),
        out_shape=jax.ShapeDtypeStruct((m, d), F32),
        compiler_params=_params(("arbitrary",)),
        name="matmul_residual",
    )(x, a, w)


def _dilated_kernel(*refs, qb, span, has_prev, emit_lse):
    q_ref, k_ref, v_ref, pcol_ref, prow_ref = refs[:5]
    pos = 5
    if has_prev:
        oprev_ref, lprev_ref = refs[pos:pos + 2]
        pos += 2
    o_ref = refs[pos]
    pos += 1
    if emit_lse:
        lse_ref = refs[pos]
        pos += 1
    kbuf, vbuf, pbuf = refs[pos:pos + 3]

    n = pl.program_id(2)

    @pl.when(n == 0)
    def _():
        kbuf[0:BLOCK, :] = jnp.zeros((BLOCK, DSA_WIDTH), BF16)
        vbuf[0:BLOCK, :] = jnp.zeros((BLOCK, DSA_WIDTH), BF16)
        pbuf[:, 0:BLOCK] = jnp.zeros((1, BLOCK), jnp.int32)

    kbuf[BLOCK:, :] = k_ref[...]
    vbuf[BLOCK:, :] = v_ref[...]
    pbuf[:, BLOCK:] = prow_ref[...]

    lane = lax.broadcasted_iota(jnp.int32, (BLOCK, LANES), 1)
    low = lane < DSA_HEAD_DIM
    row = lax.broadcasted_iota(jnp.int32, (BLOCK, 2 * BLOCK), 0)
    col = lax.broadcasted_iota(jnp.int32, (BLOCK, 2 * BLOCK), 1)
    rel = row + BLOCK - col
    band = (rel >= 0) & (rel <= span)
    first_valid = band & ((col >= BLOCK) | (n > 0))
    scale = DSA_HEAD_DIM ** -0.5

    for i in range(qb):
        rows = slice(i * BLOCK, (i + 1) * BLOCK)
        keys = slice(i * BLOCK, (i + 2) * BLOCK)
        kk = kbuf[keys, :]
        vv = vbuf[keys, :]
        pk = pbuf[:, keys]
        pq = pcol_ref[rows, :]
        dist = jnp.abs(jnp.concatenate([pq, pq], axis=1) - pk).astype(F32)
        valid = first_valid if i == 0 else band
        q = q_ref[rows, :]
        if emit_lse:
            lse_tile = jnp.zeros((BLOCK, LANES), F32)
        if has_prev:
            lprev = lprev_ref[rows, :]
        for p in range(DSA_HEADS // 2):
            cols = slice(p * LANES, (p + 1) * LANES)
            q2, k2, v2 = q[:, cols], kk[:, cols], vv[:, cols]
            pvs, c_new, c_prev = [], [], []
            for half in range(2):
                h = 2 * p + half
                slope = 2.0 ** (-8.0 * (h + 1) / DSA_HEADS)
                qm = jnp.where(low if half == 0 else ~low, q2, jnp.zeros_like(q2))
                s = lax.dot_general(qm, k2, (((1,), (1,)), ((), ())), preferred_element_type=F32)
                s = s * scale - slope * dist
                s = jnp.where(valid, s, MASK_VALUE)
                m = jnp.max(s, axis=-1, keepdims=True)
                e = jnp.exp(s - m)
                l = jnp.sum(e, axis=-1, keepdims=True)
                pvs.append(jnp.dot(e.astype(BF16), v2, preferred_element_type=F32))
                lse = m + jnp.log(l)
                if has_prev:
                    lp = lprev[:, h:h + 1]
                    mx = jnp.maximum(lp, lse)
                    wa = jnp.exp(lp - mx)
                    wb = jnp.exp(lse - mx)
                    den = wa + wb
                    c_prev.append(wa / den)
                    c_new.append(wb / (den * l))
                    lse = mx + jnp.log(den)
                else:
                    c_new.append(1.0 / l)
                if emit_lse:
                    lse_tile = jnp.where(lane == h, lse, lse_tile)
            out = jnp.where(low, pvs[0] * c_new[0], pvs[1] * c_new[1])
            if has_prev:
                out = out + oprev_ref[rows, cols].astype(F32) * jnp.where(low, c_prev[0], c_prev[1])
            o_ref[rows, cols] = out.astype(o_ref.dtype)
        if emit_lse:
            lse_ref[rows, :] = lse_tile

    tail = slice(qb * BLOCK, (qb + 1) * BLOCK)
    kbuf[0:BLOCK, :] = kbuf[tail, :]
    vbuf[0:BLOCK, :] = vbuf[tail, :]
    pbuf[:, 0:BLOCK] = pbuf[:, tail]


def _dilated_group(qkv_v, positions, prev, *, group, emit_lse, qb):
    b, s = positions.shape
    window, dil = DIL_PAIRS[group]
    span = window // dil
    sub = s // dil
    tq = qb * BLOCK
    nb = sub // tq
    pcol = jnp.broadcast_to(positions[:, :, None], (b, s, LANES)).reshape(b, sub, dil * LANES)
    prow = positions.reshape(b, sub, dil).transpose(0, 2, 1).reshape(b, dil, 1, sub)

    def qkv_spec(which):
        return pl.BlockSpec((None, tq, DSA_WIDTH), lambda bi, r, n: (bi, n, r * 3 + which))

    in_specs = [qkv_spec(0), qkv_spec(1), qkv_spec(2),
                pl.BlockSpec((None, tq, LANES), lambda bi, r, n: (bi, n, r)),
                pl.BlockSpec((None, None, 1, tq), lambda bi, r, n: (bi, r, 0, n))]
    args = [qkv_v, qkv_v, qkv_v, pcol, prow]
    if prev is not None:
        o_prev, l_prev = prev
        in_specs += [pl.BlockSpec((None, tq, DSA_WIDTH), lambda bi, r, n: (bi, n, r)),
                     pl.BlockSpec((None, tq, LANES), lambda bi, r, n: (bi, n, r))]
        args += [o_prev.reshape(b, sub, dil * DSA_WIDTH), l_prev.reshape(b, sub, dil * LANES)]
    out_specs = [pl.BlockSpec((None, tq, DSA_WIDTH), lambda bi, r, n: (bi, n, r))]
    out_shape = [jax.ShapeDtypeStruct((b, sub, dil * DSA_WIDTH), BF16)]
    if emit_lse:
        out_specs.append(pl.BlockSpec((None, tq, LANES), lambda bi, r, n: (bi, n, r)))
        out_shape.append(jax.ShapeDtypeStruct((b, sub, dil * LANES), F32))
    outs = pl.pallas_call(
        functools.partial(_dilated_kernel, qb=qb, span=span, has_prev=prev is not None,
                          emit_lse=emit_lse),
        grid=(b, dil, nb),
        in_specs=in_specs,
        out_specs=out_specs,
        out_shape=out_shape,
        scratch_shapes=[pltpu.VMEM((tq + BLOCK, DSA_WIDTH), BF16),
                        pltpu.VMEM((tq + BLOCK, DSA_WIDTH), BF16),
                        pltpu.VMEM((1, tq + BLOCK), jnp.int32)],
        compiler_params=_params(("arbitrary", "arbitrary", "arbitrary")),
        name=f"dilated_group{group}",
    )(*args)
    o = outs[0].reshape(b, s, DSA_WIDTH)
    if emit_lse:
        return o, outs[1].reshape(b, s, LANES)
    return o


def _swiglu_kernel(x_ref, g_ref, wg_ref, wu_ref, wd_ref, o_ref, h_scr):
    f = pl.program_id(1)

    @pl.when(f == 0)
    def _():
        x = x_ref[...]
        h_scr[...] = _rms(x, g_ref[...]).astype(BF16)
        o_ref[...] = x

    h = h_scr[...]
    gate = jnp.dot(h, wg_ref[...], preferred_element_type=F32)
    up = jnp.dot(h, wu_ref[...], preferred_element_type=F32)
    act = (gate * jax.nn.sigmoid(gate) * up).astype(BF16)
    o_ref[...] += jnp.dot(act, wd_ref[...], preferred_element_type=F32)


def _swiglu(x, g, wg, wu, wd, *, tm, tf):
    m, d = x.shape
    ff = wg.shape[1]
    return pl.pallas_call(
        _swiglu_kernel,
        grid=(m // tm, ff // tf),
        in_specs=[pl.BlockSpec((tm, d), lambda i, f: (i, 0)),
                  pl.BlockSpec((1, d), lambda i, f: (0, 0)),
                  pl.BlockSpec((d, tf), lambda i, f: (0, f)),
                  pl.BlockSpec((d, tf), lambda i, f: (0, f)),
                  pl.BlockSpec((tf, d), lambda i, f: (f, 0))],
        out_specs=pl.BlockSpec((tm, d), lambda i, f: (i, 0)),
        out_shape=jax.ShapeDtypeStruct((m, d), F32),
        scratch_shapes=[pltpu.VMEM((tm, d), BF16)],
        compiler_params=_params(("arbitrary", "arbitrary")),
        name="swiglu",
    )(x, g.reshape(1, d), wg, wu, wd)


def _rope_lanes(x, c, sa, sb):
    return x * c + pltpu.roll(x, LANES - QK_ROPE // 2, 1) * sa + pltpu.roll(x, QK_ROPE // 2, 1) * sb


def _mla_project_kernel(x_ref, g_ref, win_ref, gq_ref, gkv_ref, wq_ref, wk_ref, wkr_ref, wv_ref,
                        c_ref, sa_ref, sb_ref, q_ref, k_ref, v_ref, *, scale):
    h = _rms(x_ref[...], g_ref[...]).astype(BF16)
    a = jnp.dot(h, win_ref[...], preferred_element_type=F32)
    c_q = _rms(a[:, :Q_LORA], gq_ref[...]).astype(BF16)
    c_kv = _rms(a[:, Q_LORA:Q_LORA + KV_LORA], gkv_ref[...]).astype(BF16)
    k_rope = a[:, Q_LORA + KV_LORA:].astype(BF16)
    q = jnp.dot(c_q, wq_ref[...], preferred_element_type=F32)
    k = (jnp.dot(c_kv, wk_ref[...], preferred_element_type=F32)
         + jnp.dot(k_rope, wkr_ref[...], preferred_element_type=F32))
    v = jnp.dot(c_kv, wv_ref[...], preferred_element_type=F32)
    c, sa, sb = c_ref[...], sa_ref[...], sb_ref[...]
    ones_lane = lax.broadcasted_iota(jnp.int32, (1, 2 * LANES), 1) % LANES == V_DIM
    for p in range(MLA_HEADS // 2):
        qs, ks = [], []
        for half in range(2):
            cols = slice((2 * p + half) * LANES, (2 * p + half + 1) * LANES)
            qs.append((_rope_lanes(q[:, cols], c, sa, sb) * scale).astype(BF16))
            ks.append(_rope_lanes(k[:, cols], c, sa, sb).astype(BF16))
        q_ref[p] = jnp.concatenate(qs, axis=1)
        k_ref[p] = jnp.concatenate(ks, axis=1)
        v_ref[p] = jnp.where(ones_lane, 1.0, v[:, 2 * p * LANES:2 * (p + 1) * LANES]).astype(BF16)


def _mla_project(x, g, w_in, gq, gkv, wq_pad, wk_pad, wkr, wv, tables, *, tm):
    b, s, d = x.shape
    c, sa, sb = tables
    pairs = MLA_HEADS // 2
    scale = (QK_NOPE + QK_ROPE) ** -0.5 * LOG2_E

    def full(arr):
        return pl.BlockSpec(arr.shape, lambda bi, i: (0,) * arr.ndim)

    tab = pl.BlockSpec((None, tm, LANES), lambda bi, i: (bi, i, 0))
    ops = [g.reshape(1, d), w_in, gq.reshape(1, -1), gkv.reshape(1, -1), wq_pad, wk_pad, wkr, wv]
    return pl.pallas_call(
        functools.partial(_mla_project_kernel, scale=scale),
        grid=(b, s // tm),
        in_specs=[pl.BlockSpec((None, tm, d), lambda bi, i: (bi, i, 0))] + [full(o) for o in ops]
                 + [tab, tab, tab],
        out_specs=[pl.BlockSpec((None, pairs, tm, 2 * LANES), lambda bi, i: (bi, 0, i, 0)),
                   pl.BlockSpec((None, pairs, tm, 2 * LANES), lambda bi, i: (bi, 0, i, 0)),
                   pl.BlockSpec((None, pairs, tm, 2 * LANES), lambda bi, i: (bi, 0, i, 0))],
        out_shape=[jax.ShapeDtypeStruct((b, pairs, s, 2 * LANES), BF16)] * 3,
        compiler_params=_params(("arbitrary", "arbitrary")),
        name="mla_project",
    )(x, *ops, c, sa, sb)


def _flash_kernel(q_ref, k_ref, v_ref, o_ref, s_scr, m_scr, acc_scr, *, tq, tk):
    qi = pl.program_id(2)
    m_scr[...] = jnp.full(m_scr.shape, MASK_VALUE, F32)
    acc_scr[...] = jnp.zeros(acc_scr.shape, F32)
    nt = (((1,), (1,)), ((), ()))

    def key_rows(kj):
        return pl.ds(pl.multiple_of(kj * tk, tk), tk)

    def head_cols(half):
        return slice(half * LANES, (half + 1) * LANES)

    def scores(kj, slot):
        ks = k_ref[key_rows(kj), :]
        for half in range(2):
            s_scr[slot, half] = lax.dot_general(q_ref[:, head_cols(half)], ks[:, head_cols(half)], nt,
                                                preferred_element_type=F32)

    def update(kj, rows, s_pair, triangular):
        n = rows.stop - rows.start
        vs = v_ref[key_rows(kj), :]
        for half in range(2):
            s = s_pair[half]
            if triangular:
                row = lax.broadcasted_iota(jnp.int32, (n, tk), 0)
                col = lax.broadcasted_iota(jnp.int32, (n, tk), 1)
                s = jnp.where(row >= col, s, MASK_VALUE)
            m_prev = m_scr[half, rows]
            m_new = jnp.maximum(m_prev, jnp.max(s, axis=-1, keepdims=True))
            e = jnp.exp2(s - jnp.tile(m_new, (1, tk // LANES)))
            m_scr[half, rows] = m_new
            pv = jnp.dot(e.astype(BF16), vs[:, head_cols(half)], preferred_element_type=F32)
            acc_scr[half, rows] = acc_scr[half, rows] * jnp.exp2(m_prev - m_new) + pv

    everything = slice(0, tq)
    top, bottom = slice(0, tk), slice(tk, tq)

    def from_scratch(slot, rows):
        return [s_scr[slot, half, rows] for half in range(2)]

    scores(0, 0)

    def body(jj, carry):
        scores(2 * jj + 1, 1)
        update(2 * jj, everything, from_scratch(0, everything), False)
        scores(2 * jj + 2, 0)
        update(2 * jj + 1, everything, from_scratch(1, everything), False)
        return carry

    lax.fori_loop(0, qi, body, 0)
    update(2 * qi, top, from_scratch(0, top), True)
    update(2 * qi, bottom, from_scratch(0, bottom), False)
    ks = k_ref[key_rows(2 * qi + 1), :]
    last = [lax.dot_general(q_ref[bottom, head_cols(half)], ks[:, head_cols(half)], nt,
                            preferred_element_type=F32) for half in range(2)]
    update(2 * qi + 1, bottom, last, True)
    out = [acc_scr[half] / acc_scr[half][:, V_DIM:V_DIM + 1] for half in range(2)]
    low = lax.broadcasted_iota(jnp.int32, (tq, LANES), 1) < V_DIM
    o_ref[...] = jnp.where(low, out[0], pltpu.roll(out[1], V_DIM, 1)).astype(o_ref.dtype)


def _flash_attention(q, k, v, *, tk):
    b, pairs, s, _ = q.shape
    tq = 2 * tk
    return pl.pallas_call(
        functools.partial(_flash_kernel, tq=tq, tk=tk),
        grid=(b, pairs, s // tq),
        in_specs=[pl.BlockSpec((None, None, tq, 2 * LANES), lambda bi, p, i: (bi, p, i, 0)),
                  pl.BlockSpec((None, None, s, 2 * LANES), lambda bi, p, i: (bi, p, 0, 0),
                               pipeline_mode=pl.Buffered(1)),
                  pl.BlockSpec((None, None, s, 2 * LANES), lambda bi, p, i: (bi, p, 0, 0),
                               pipeline_mode=pl.Buffered(1))],
        out_specs=pl.BlockSpec((None, tq, LANES), lambda bi, p, i: (bi, i, p)),
        out_shape=jax.ShapeDtypeStruct((b, s, pairs * LANES), BF16),
        scratch_shapes=[pltpu.VMEM((2, 2, tq, tk), F32),
                        pltpu.VMEM((2, tq, LANES), F32), pltpu.VMEM((2, tq, LANES), F32)],
        compiler_params=_params(("arbitrary", "arbitrary", "arbitrary")),
        name="flash_attention",
    )(q, k, v)


def _route_kernel(x_ref, g_ref, wr_ref, meta_ref, cnt_ref, carry):
    i = pl.program_id(0)
    t = x_ref.shape[0]

    @pl.when(i == 0)
    def _():
        carry[...] = jnp.zeros(carry.shape, F32)

    h = _rms(x_ref[...], g_ref[...])
    logits = jnp.dot(h, wr_ref[...], preferred_element_type=F32, precision=lax.Precision.HIGHEST)
    lane = lax.broadcasted_iota(jnp.int32, (t, LANES), 1)
    lg = jnp.where(lane < N_EXPERTS, logits, -jnp.inf)
    m1 = jnp.max(lg, axis=-1, keepdims=True)
    i1 = jnp.min(jnp.where(lg == m1, lane, LANES), axis=-1, keepdims=True)
    lg2 = jnp.where(lane == i1, -jnp.inf, lg)
    m2 = jnp.max(lg2, axis=-1, keepdims=True)
    i2 = jnp.min(jnp.where(lg2 == m2, lane, LANES), axis=-1, keepdims=True)
    e2 = jnp.exp(m2 - m1)
    w1 = 1.0 / (1.0 + e2)
    w2 = e2 / (1.0 + e2)
    hot1 = lane == i1
    hot2 = lane == i2
    onehot = (hot1 | hot2).astype(BF16)
    r = lax.broadcasted_iota(jnp.int32, (t, t), 0)
    c = lax.broadcasted_iota(jnp.int32, (t, t), 1)
    lower = (c < r).astype(BF16)
    before = jnp.dot(lower, onehot, preferred_element_type=F32) + carry[...]
    rank1 = jnp.sum(jnp.where(hot1, before, 0.0), axis=-1, keepdims=True)
    rank2 = jnp.sum(jnp.where(hot2, before, 0.0), axis=-1, keepdims=True)
    carry[...] += jnp.sum(onehot.astype(F32), axis=0, keepdims=True)
    meta = jnp.zeros((t, LANES), F32)
    for k, val in enumerate((i1.astype(F32), i2.astype(F32), rank1, rank2, w1, w2)):
        meta = jnp.where(lane == k, val, meta)
    meta_ref[...] = meta
    cnt_ref[...] = carry[...]


def _route(x, g, w_router, *, tm):
    m, d = x.shape
    wr = jnp.zeros((d, LANES), F32).at[:, :N_EXPERTS].set(w_router)
    return pl.pallas_call(
        _route_kernel,
        grid=(m // tm,),
        in_specs=[pl.BlockSpec((tm, d), lambda i: (i, 0)),
                  pl.BlockSpec((1, d), lambda i: (0, 0)),
                  pl.BlockSpec((d, LANES), lambda i: (0, 0))],
        out_specs=[pl.BlockSpec((tm, LANES), lambda i: (i, 0)),
                   pl.BlockSpec((1, LANES), lambda i: (0, 0))],
        out_shape=[jax.ShapeDtypeStruct((m, LANES), F32), jax.ShapeDtypeStruct((1, LANES), F32)],
        scratch_shapes=[pltpu.VMEM((1, LANES), F32)],
        compiler_params=_params(("arbitrary",)),
        name="route",
    )(x, g.reshape(1, d), wr)


def _dispatch_kernel(dest_ref, x_ref, g_ref, init_ref, xs_ref, h_scr, sem):
    del init_ref
    t = x_ref.shape[0]
    h_scr[...] = _rms(x_ref[...], g_ref[...])

    def row_copy(j, slot):
        return pltpu.make_async_copy(h_scr.at[pl.ds(j, 1)],
                                     xs_ref.at[pl.ds(dest_ref[2 * j + slot], 1)], sem)

    def start(j, carry):
        row_copy(j, 0).start()
        row_copy(j, 1).start()
        return carry

    lax.fori_loop(0, t, start, 0, unroll=8)
    for _ in range(2):
        pltpu.make_async_copy(h_scr, xs_ref.at[pl.ds(0, t)], sem).wait()


def _dispatch(x, g, dest, rows, *, tm):
    m, d = x.shape
    init = jnp.zeros((rows, d), F32)
    return pl.pallas_call(
        _dispatch_kernel,
        grid=(m // tm,),
        in_specs=[pl.BlockSpec((2 * tm,), lambda i: (i,), memory_space=pltpu.SMEM),
                  pl.BlockSpec((tm, d), lambda i: (i, 0)),
                  pl.BlockSpec((1, d), lambda i: (0, 0)),
                  pl.BlockSpec(memory_space=pl.ANY)],
        out_specs=pl.BlockSpec(memory_space=pl.ANY),
        out_shape=jax.ShapeDtypeStruct((rows, d), F32),
        scratch_shapes=[pltpu.VMEM((tm, d), F32), pltpu.SemaphoreType.DMA(())],
        input_output_aliases={3: 0},
        compiler_params=_params(("arbitrary",)),
        name="dispatch",
    )(dest, x, g.reshape(1, d), init)


def _grouped_kernel(te_ref, tv_ref, xs_ref, wg_ref, wu_ref, wd_ref, y_ref):
    i = pl.program_id(0)
    f = pl.program_id(1)

    @pl.when(f == 0)
    def _():
        y_ref[...] = jnp.zeros(y_ref.shape, F32)

    @pl.when(tv_ref[i] > 0)
    def _():
        h = xs_ref[...].astype(BF16)
        gate = jnp.dot(h, wg_ref[...], preferred_element_type=F32)
        up = jnp.dot(h, wu_ref[...], preferred_element_type=F32)
        act = (gate * jax.nn.sigmoid(gate) * up).astype(BF16)
        y_ref[...] += jnp.dot(act, wd_ref[...], preferred_element_type=F32)


def _grouped_swiglu(xs, tile_expert, tile_valid, wg, wu, wd, *, tm, tf):
    rows, d = xs.shape
    ff = wg.shape[2]
    grid_spec = pltpu.PrefetchScalarGridSpec(
        num_scalar_prefetch=2,
        grid=(rows // tm, ff // tf),
        in_specs=[pl.BlockSpec((tm, d), lambda i, f, te, tv: (i, 0)),
                  pl.BlockSpec((None, d, tf), lambda i, f, te, tv: (te[i], 0, f)),
                  pl.BlockSpec((None, d, tf), lambda i, f, te, tv: (te[i], 0, f)),
                  pl.BlockSpec((None, tf, d), lambda i, f, te, tv: (te[i], f, 0))],
        out_specs=pl.BlockSpec((tm, d), lambda i, f, te, tv: (i, 0)),
    )
    return pl.pallas_call(
        _grouped_kernel,
        grid_spec=grid_spec,
        out_shape=jax.ShapeDtypeStruct((rows, d), F32),
        compiler_params=_params(("arbitrary", "arbitrary")),
        name="grouped_swiglu",
    )(tile_expert, tile_valid, xs, wg, wu, wd)


def _combine_kernel(dest_ref, x_ref, meta_ref, g_ref, y_ref, o_ref, y0_scr, y1_scr, sem):
    t = x_ref.shape[0]

    def row_copy(j, slot):
        dst = y0_scr if slot == 0 else y1_scr
        return pltpu.make_async_copy(y_ref.at[pl.ds(dest_ref[2 * j + slot], 1)],
                                     dst.at[pl.ds(j, 1)], sem)

    def start(j, carry):
        row_copy(j, 0).start()
        row_copy(j, 1).start()
        return carry

    lax.fori_loop(0, t, start, 0, unroll=8)
    for dst in (y0_scr, y1_scr):
        pltpu.make_async_copy(y_ref.at[pl.ds(0, t)], dst, sem).wait()
    meta = meta_ref[...]
    x = x_ref[...] + (meta[:, 4:5] * y0_scr[...] + meta[:, 5:6] * y1_scr[...])
    o_ref[...] = _rms(x, g_ref[...])


def _combine(x, meta, dest, y, g, *, tm):
    m, d = x.shape
    return pl.pallas_call(
        _combine_kernel,
        grid=(m // tm,),
        in_specs=[pl.BlockSpec((2 * tm,), lambda i: (i,), memory_space=pltpu.SMEM),
                  pl.BlockSpec((tm, d), lambda i: (i, 0)),
                  pl.BlockSpec((tm, LANES), lambda i: (i, 0)),
                  pl.BlockSpec((1, d), lambda i: (0, 0)),
                  pl.BlockSpec(memory_space=pl.ANY)],
        out_specs=pl.BlockSpec((tm, d), lambda i: (i, 0)),
        out_shape=jax.ShapeDtypeStruct((m, d), F32),
        scratch_shapes=[pltpu.VMEM((tm, d), F32), pltpu.VMEM((tm, d), F32),
                        pltpu.SemaphoreType.DMA(())],
        compiler_params=_params(("arbitrary",)),
        name="combine",
    )(dest, x, meta, g.reshape(1, d), y)


def _moe_layer(x, g, w_router, wg, wu, wd, final_g, *, tm, tf):
    m, d = x.shape
    meta, counts = _route(x, g, w_router, tm=512)
    cnt = counts[0, :N_EXPERTS].astype(jnp.int32)
    tiles = (cnt + tm - 1) // tm
    tile_end = jnp.cumsum(tiles)
    offset = (tile_end - tiles) * tm
    ids = meta[:, 0:2].astype(jnp.int32)
    expert_ids = jnp.arange(N_EXPERTS, dtype=jnp.int32)
    base = jnp.sum(jnp.where(ids[:, :, None] == expert_ids, offset, 0), axis=-1)
    dest = (base + meta[:, 2:4].astype(jnp.int32)).reshape(2 * m)
    n_tiles = (2 * m) // tm + N_EXPERTS
    tile_id = jnp.arange(n_tiles, dtype=jnp.int32)
    tile_expert = jnp.sum(tile_id[:, None] >= tile_end[None, :], axis=1).astype(jnp.int32)
    tile_valid = (tile_expert < N_EXPERTS).astype(jnp.int32)
    tile_expert = jnp.minimum(tile_expert, N_EXPERTS - 1)
    xs = _dispatch(x, g, dest, n_tiles * tm, tm=256)
    y = _grouped_swiglu(xs, tile_expert, tile_valid, wg, wu, wd, tm=tm, tf=tf)
    return _combine(x, meta, dest, y, final_g, tm=256)


def _mla_weights(w_q_up, w_kv_up):
    hd = QK_NOPE + QK_ROPE
    wq = w_q_up.reshape(Q_LORA, MLA_HEADS, hd)
    wq_pad = jnp.pad(wq, ((0, 0), (0, 0), (0, LANES - hd))).reshape(Q_LORA, MLA_HEADS * LANES)
    wkv = w_kv_up.reshape(KV_LORA, MLA_HEADS, QK_NOPE + V_DIM)
    wk_pad = jnp.pad(wkv[:, :, :QK_NOPE], ((0, 0), (0, 0), (0, LANES - QK_NOPE)))
    wk_pad = wk_pad.reshape(KV_LORA, MLA_HEADS * LANES)
    wv = jnp.pad(wkv[:, :, QK_NOPE:], ((0, 0), (0, 0), (0, LANES - V_DIM))).reshape(KV_LORA, MLA_HEADS * LANES)
    place = jnp.zeros((QK_ROPE, LANES), F32).at[jnp.arange(QK_ROPE), QK_NOPE + jnp.arange(QK_ROPE)].set(1.0)
    wkr = jnp.tile(place, (1, MLA_HEADS))
    return wq_pad.astype(BF16), wk_pad.astype(BF16), wkr.astype(BF16), wv.astype(BF16)


def _rope_lane_tables(positions):
    half = QK_ROPE // 2
    inv = ROPE_THETA ** (-jnp.arange(0, QK_ROPE, 2, dtype=F32) / QK_ROPE)
    ang = positions.astype(F32)[..., None] * inv
    cos, sin = jnp.cos(ang), jnp.sin(ang)
    shape = positions.shape
    ones = jnp.ones(shape + (QK_NOPE,), F32)
    zeros_n = jnp.zeros(shape + (QK_NOPE,), F32)
    zeros_h = jnp.zeros(shape + (half,), F32)
    pad = jnp.zeros(shape + (LANES - QK_NOPE - QK_ROPE,), F32)
    c = jnp.concatenate([ones, cos, cos, pad], axis=-1)
    sa = jnp.concatenate([zeros_n, -sin, zeros_h, pad], axis=-1)
    sb = jnp.concatenate([zeros_n, zeros_h, sin, pad], axis=-1)
    return c, sa, sb


def kernel(x, positions, mix_norm_g, ffn_norm_g, a_w_qkv, a_w_out, b_w_in, b_q_norm_g, b_kv_norm_g,
           b_w_q_up, b_w_kv_up, b_w_out, ffn_w_gate, ffn_w_up, ffn_w_down, moe_w_router, moe_w_gate,
           moe_w_up, moe_w_down, final_norm_g):
    b, s, d = x.shape
    m = b * s
    xf = x.reshape(m, d)

    w_qkv = a_w_qkv[0].astype(BF16)
    group_cols = 3 * DSA_WIDTH
    state = None
    for group, (_, dil) in enumerate(DIL_PAIRS):
        w_group = w_qkv[:, group * group_cols:(group + 1) * group_cols]
        qkv_v = _qkv_project(xf, mix_norm_g[0], w_group, dil=dil, tm=1024 if dil == 1 else 512)
        qkv_v = qkv_v.reshape(b, s // dil, dil * group_cols)
        state = _dilated_group(qkv_v, positions, state, group=group,
                               emit_lse=group < len(DIL_PAIRS) - 1, qb=4)
    mixed = state
    x1 = _matmul_residual(xf, mixed.reshape(m, DSA_WIDTH), a_w_out[0].astype(BF16), tm=1024)
    x2 = _swiglu(x1, ffn_norm_g[0], ffn_w_gate[0].astype(BF16), ffn_w_up[0].astype(BF16),
                 ffn_w_down[0].astype(BF16), tm=1024, tf=256)

    wq_pad, wk_pad, wkr, wv = _mla_weights(b_w_q_up[0], b_w_kv_up[0])
    q, k, v = _mla_project(x2.reshape(b, s, d), mix_norm_g[1], b_w_in[0].astype(BF16), b_q_norm_g[0],
                           b_kv_norm_g[0], wq_pad, wk_pad, wkr, wv, _rope_lane_tables(positions), tm=512)
    attn = _flash_attention(q, k, v, tk=512)
    x3 = _matmul_residual(x2, attn.reshape(m, MLA_HEADS * V_DIM), b_w_out[0].astype(BF16), tm=1024)
    out = _moe_layer(x3, ffn_norm_g[1], moe_w_router[0], moe_w_gate[0].astype(BF16),
                     moe_w_up[0].astype(BF16), moe_w_down[0].astype(BF16), final_norm_g,
                     tm=1024, tf=512)
    return out.reshape(b, s, d)
```

```python
import functools

import jax
import jax.numpy as jnp
from jax import lax
from jax.experimental import pallas as pl
from jax.experimental.pallas import tpu as pltpu

F32 = jnp.float32
BF16 = jnp.bfloat16

RMS_EPS = 1e-6
BLOCK = 128
LANES = 128
DIL_PAIRS = ((128, 1), (512, 4), (2048, 16))
DSA_HEADS = 8
DSA_HEAD_DIM = 64
DSA_WIDTH = DSA_HEADS * DSA_HEAD_DIM
MLA_HEADS = 16
QK_NOPE = 64
QK_ROPE = 32
V_DIM = 64
Q_LORA = 384
KV_LORA = 256
ROPE_THETA = 10000.0
N_EXPERTS = 8
MASK_VALUE = -1e30
LOG2_E = 1.4426950408889634
VMEM_LIMIT = 56 << 20
GROUPED_VMEM_LIMIT = 60 << 20


def _params(semantics, vmem=VMEM_LIMIT):
    return pltpu.CompilerParams(dimension_semantics=semantics, vmem_limit_bytes=vmem)


def _rms(x, g):
    ms = jnp.mean(x * x, axis=-1, keepdims=True)
    return x * lax.rsqrt(ms + RMS_EPS) * g


def _residue_slot(dil):
    if dil == 16:
        return tuple(4 * (r % 4) + r // 4 for r in range(dil))
    return tuple(range(dil))


def _deinterleave(tm, dil):
    n = tm // dil
    out_row = jnp.arange(tm, dtype=jnp.int32)
    source = (out_row % n) * dil + out_row // n
    return (source[:, None] == out_row[None, :]).astype(BF16)


def _qkv_kernel(x_ref, g_ref, perm_ref, w_ref, o_ref, *, dil):
    tm = x_ref.shape[0]
    n = tm // dil
    width = w_ref.shape[1]
    h = _rms(x_ref[...], g_ref[...]).astype(BF16)
    if dil > 1:
        h = jnp.dot(perm_ref[...], h, preferred_element_type=F32).astype(BF16)
    res = jnp.dot(h, w_ref[...], preferred_element_type=F32)
    for r, slot in enumerate(_residue_slot(dil)):
        o_ref[:, slot * width:(slot + 1) * width] = res[r * n:(r + 1) * n, :].astype(o_ref.dtype)


def _qkv_project(x, g, w, *, dil, tm):
    m, d = x.shape
    width = w.shape[1]
    n = tm // dil
    perm = _deinterleave(tm, dil)
    return pl.pallas_call(
        functools.partial(_qkv_kernel, dil=dil),
        grid=(m // tm,),
        in_specs=[pl.BlockSpec((tm, d), lambda i: (i, 0)),
                  pl.BlockSpec((1, d), lambda i: (0, 0)),
                  pl.BlockSpec((tm, tm), lambda i: (0, 0)),
                  pl.BlockSpec((d, width), lambda i: (0, 0))],
        out_specs=pl.BlockSpec((n, dil * width), lambda i: (i, 0)),
        out_shape=jax.ShapeDtypeStruct((m // dil, dil * width), BF16),
        compiler_params=_params(("arbitrary",)),
        name=f"qkv_project_d{dil}",
    )(x, g.reshape(1, d), perm, w)


def _matmul_residual_kernel(x_ref, a_ref, w_ref, o_ref):
    o_ref[...] = x_ref[...] + jnp.dot(a_ref[...], w_ref[...], preferred_element_type=F32)


def _matmul_residual(x, a, w, *, tm):
    m, d = x.shape
    k = a.shape[1]
    return pl.pallas_call(
        _matmul_residual_kernel,
        grid=(m // tm,),
        in_specs=[pl.BlockSpec((tm, d), lambda i: (i, 0)),
                  pl.BlockSpec((tm, k), lambda i: (i, 0)),
                  pl.BlockSpec((k, d), lambda i: (0, 0))],
        out_specs=pl.BlockSpec((tm, d), lambda i: (i, 0)),
        out_shape=jax.ShapeDtypeStruct((m, d), F32),
        compiler_params=_params(("arbitrary",)),
        name="matmul_residual",
    )(x, a, w)


def _merge_kernel(*refs, qb, span, has_prev, emit_lse):
    q_ref, k_ref, v_ref, prow_ref = refs[:4]
    pos = 4
    if has_prev:
        oprev_ref, lprev_ref = refs[pos:pos + 2]
        pos += 2
    if emit_lse:
        perm_ref, o_ref, lse_ref = refs[pos:pos + 3]
        pos += 3
    else:
        o_ref = refs[pos]
        pos += 1
    kbuf, vbuf, pbuf = refs[pos:pos + 3]
    if emit_lse:
        o_scr, lse_scr = refs[pos + 3:pos + 5]

    n = pl.program_id(2)
    tq = qb * BLOCK

    @pl.when(n == 0)
    def _():
        kbuf[0:BLOCK, :] = jnp.zeros((BLOCK, DSA_WIDTH), BF16)
        vbuf[0:BLOCK, :] = jnp.zeros((BLOCK, DSA_WIDTH), BF16)
        pbuf[:, 0:BLOCK] = jnp.zeros((1, BLOCK), jnp.int32)

    kbuf[BLOCK:, :] = k_ref[...]
    vbuf[BLOCK:, :] = v_ref[...]
    pbuf[:, BLOCK:] = prow_ref[...]

    lane = lax.broadcasted_iota(jnp.int32, (BLOCK, LANES), 1)
    sub = lax.broadcasted_iota(jnp.int32, (BLOCK, LANES), 0)
    low = lane < DSA_HEAD_DIM
    diag = lane == sub
    row = lax.broadcasted_iota(jnp.int32, (BLOCK, 2 * BLOCK), 0)
    col = lax.broadcasted_iota(jnp.int32, (BLOCK, 2 * BLOCK), 1)
    rel = row + BLOCK - col
    band = (rel >= 0) & (rel <= span)
    first_valid = band & ((col >= BLOCK) | (n > 0))
    spread = (lax.broadcasted_iota(jnp.int32, (LANES, DSA_WIDTH), 0)
              == lax.broadcasted_iota(jnp.int32, (LANES, DSA_WIDTH), 1) // DSA_HEAD_DIM).astype(BF16)

    def per_head(c):
        hi = c.astype(BF16)
        lo = (c - hi.astype(F32)).astype(BF16)
        return (jnp.dot(hi, spread, preferred_element_type=F32)
                + jnp.dot(lo, spread, preferred_element_type=F32))

    def lane_to_column(v):
        return jnp.sum(jnp.where(diag, v, 0.0), axis=1, keepdims=True)

    for i in range(qb):
        rows = slice(i * BLOCK, (i + 1) * BLOCK)
        keys = slice(i * BLOCK, (i + 2) * BLOCK)
        kk = kbuf[keys, :]
        vv = vbuf[keys, :]
        pk = pbuf[:, keys]
        pq_row = pbuf[:, (i + 1) * BLOCK:(i + 2) * BLOCK]
        pq = ((lane_to_column((pq_row >> 12).astype(F32)).astype(jnp.int32) << 12)
              | lane_to_column((pq_row & 4095).astype(F32)).astype(jnp.int32))
        dist = jnp.abs(pq - pk).astype(F32)
        neg_dist = jnp.where(first_valid if i == 0 else band, -dist, MASK_VALUE)
        q = q_ref[rows, :]
        m_tile = jnp.zeros((BLOCK, LANES), F32)
        l_tile = jnp.ones((BLOCK, LANES), F32)
        pv_pairs = []
        for p in range(DSA_HEADS // 2):
            cols = slice(p * LANES, (p + 1) * LANES)
            q2, k2, v2 = q[:, cols], kk[:, cols], vv[:, cols]
            pvs = []
            for half in range(2):
                h = 2 * p + half
                slope = 2.0 ** (-8.0 * (h + 1) / DSA_HEADS)
                qm = jnp.where(low if half == 0 else ~low, q2, jnp.zeros_like(q2))
                s = lax.dot_general(qm, k2, (((1,), (1,)), ((), ())), preferred_element_type=F32)
                s = s + slope * neg_dist
                m = jnp.max(s, axis=-1, keepdims=True)
                e = jnp.exp(s - m)
                m_tile = jnp.where(lane == h, m, m_tile)
                l_tile = jnp.where(lane == h, jnp.sum(e, axis=-1, keepdims=True), l_tile)
                pvs.append(jnp.dot(e.astype(BF16), v2, preferred_element_type=F32))
            pv_pairs.append(jnp.where(low, pvs[0], pvs[1]))
        pv_all = jnp.concatenate(pv_pairs, axis=1)
        lse = m_tile + jnp.log(l_tile)
        if has_prev:
            lp = lprev_ref[rows, :]
            mx = jnp.maximum(lp, lse)
            wa = jnp.exp(lp - mx)
            wb = jnp.exp(lse - mx)
            den = wa + wb
            c_new = wb / (den * l_tile)
            lse = mx + jnp.log(den)
            out = pv_all * per_head(c_new) + oprev_ref[rows, :].astype(F32) * per_head(wa / den)
        else:
            out = pv_all * per_head(1.0 / l_tile)
        if emit_lse:
            o_scr[rows, :] = out.astype(BF16)
            lse_scr[rows, :] = lse
        else:
            o_ref[rows, :] = out.astype(o_ref.dtype)

    if emit_lse:
        part = tq // 4
        o_perm = jnp.dot(perm_ref[...], o_scr[...], preferred_element_type=F32)
        for r in range(4):
            o_ref[:, r * DSA_WIDTH:(r + 1) * DSA_WIDTH] = o_perm[r * part:(r + 1) * part, :].astype(o_ref.dtype)
            lse_ref[:, r * LANES:(r + 1) * LANES] = lse_scr[pl.ds(r, part, stride=4), :]

    tail = slice(qb * BLOCK, (qb + 1) * BLOCK)
    kbuf[0:BLOCK, :] = kbuf[tail, :]
    vbuf[0:BLOCK, :] = vbuf[tail, :]
    pbuf[:, 0:BLOCK] = pbuf[:, tail]


def _merge_group(qkv_v, positions, prev, *, group, qb):
    b, s = positions.shape
    window, dil = DIL_PAIRS[group]
    last = group == len(DIL_PAIRS) - 1
    span = window // dil
    sub = s // dil
    tq = qb * BLOCK
    nb = sub // tq
    residue_of_slot = sorted(range(dil), key=_residue_slot(dil).__getitem__)
    prow = positions.reshape(b, sub, dil).transpose(0, 2, 1)[:, jnp.asarray(residue_of_slot), :]
    prow = prow.reshape(b, dil, 1, sub)

    def qkv_spec(which):
        return pl.BlockSpec((None, tq, DSA_WIDTH), lambda bi, r, n: (bi, n, r * 3 + which))

    in_specs = [qkv_spec(0), qkv_spec(1), qkv_spec(2),
                pl.BlockSpec((None, None, 1, tq), lambda bi, r, n: (bi, r, 0, n))]
    args = [qkv_v, qkv_v, qkv_v, prow]
    if prev is not None:
        in_specs += [pl.BlockSpec((None, tq, DSA_WIDTH), lambda bi, r, n: (bi, n, r)),
                     pl.BlockSpec((None, tq, LANES), lambda bi, r, n: (bi, n, r))]
        args += list(prev)
    scratch = [pltpu.VMEM((tq + BLOCK, DSA_WIDTH), BF16), pltpu.VMEM((tq + BLOCK, DSA_WIDTH), BF16),
               pltpu.VMEM((1, tq + BLOCK), jnp.int32)]
    if last:
        out_specs = [pl.BlockSpec((None, tq, DSA_WIDTH), lambda bi, r, n: (bi, n, r))]
        out_shape = [jax.ShapeDtypeStruct((b, sub, dil * DSA_WIDTH), BF16)]
    else:
        in_specs.append(pl.BlockSpec((tq, tq), lambda bi, r, n: (0, 0)))
        args.append(_deinterleave(tq, 4))
        out_specs = [pl.BlockSpec((None, tq // 4, 4 * DSA_WIDTH), lambda bi, r, n: (bi, n, r)),
                     pl.BlockSpec((None, tq // 4, 4 * LANES), lambda bi, r, n: (bi, n, r))]
        out_shape = [jax.ShapeDtypeStruct((b, sub // 4, 4 * dil * DSA_WIDTH), BF16),
                     jax.ShapeDtypeStruct((b, sub // 4, 4 * dil * LANES), F32)]
        scratch += [pltpu.VMEM((tq, DSA_WIDTH), BF16), pltpu.VMEM((tq, LANES), F32)]
    outs = pl.pallas_call(
        functools.partial(_merge_kernel, qb=qb, span=span, has_prev=prev is not None, emit_lse=not last),
        grid=(b, dil, nb),
        in_specs=in_specs,
        out_specs=out_specs,
        out_shape=out_shape,
        scratch_shapes=scratch,
        compiler_params=_params(("arbitrary", "arbitrary", "arbitrary")),
        name=f"dilated_group{group}",
    )(*args)
    return outs[0] if last else tuple(outs)


def _unpermute_project_kernel(x_ref, a_ref, perm_ref, w_ref, o_ref, *, dil):
    width = w_ref.shape[0]
    stacked = jnp.concatenate([a_ref[:, slot * width:(slot + 1) * width] for slot in _residue_slot(dil)],
                              axis=0)
    tokens = jnp.dot(perm_ref[...], stacked, preferred_element_type=F32).astype(BF16)
    o_ref[...] = x_ref[...] + jnp.dot(tokens, w_ref[...], preferred_element_type=F32)


def _unpermute_project(x, a, w, *, dil, tm):
    m, d = x.shape
    width = w.shape[0]
    return pl.pallas_call(
        functools.partial(_unpermute_project_kernel, dil=dil),
        grid=(m // tm,),
        in_specs=[pl.BlockSpec((tm, d), lambda i: (i, 0)),
                  pl.BlockSpec((tm // dil, dil * width), lambda i: (i, 0)),
                  pl.BlockSpec((tm, tm), lambda i: (0, 0)),
                  pl.BlockSpec((width, d), lambda i: (0, 0))],
        out_specs=pl.BlockSpec((tm, d), lambda i: (i, 0)),
        out_shape=jax.ShapeDtypeStruct((m, d), F32),
        compiler_params=_params(("arbitrary",)),
        name="unpermute_project",
    )(x, a.reshape(m // dil, dil * width), _deinterleave(tm, dil).T, w)


def _swiglu_kernel(x_ref, g_ref, wg_ref, wu_ref, wd_ref, o_ref):
    x = x_ref[...]
    h = _rms(x, g_ref[...]).astype(BF16)
    gate = jnp.dot(h, wg_ref[...], preferred_element_type=F32)
    up = jnp.dot(h, wu_ref[...], preferred_element_type=F32)
    act = (gate * jax.nn.sigmoid(gate) * up).astype(BF16)
    o_ref[...] = x + jnp.dot(act, wd_ref[...], preferred_element_type=F32)


def _swiglu(x, g, wg, wu, wd, *, tm):
    m, d = x.shape
    ff = wg.shape[1]
    resident = pl.Buffered(1)
    return pl.pallas_call(
        _swiglu_kernel,
        grid=(m // tm,),
        in_specs=[pl.BlockSpec((tm, d), lambda i: (i, 0)),
                  pl.BlockSpec((1, d), lambda i: (0, 0)),
                  pl.BlockSpec((d, ff), lambda i: (0, 0), pipeline_mode=resident),
                  pl.BlockSpec((d, ff), lambda i: (0, 0), pipeline_mode=resident),
                  pl.BlockSpec((ff, d), lambda i: (0, 0), pipeline_mode=resident)],
        out_specs=pl.BlockSpec((tm, d), lambda i: (i, 0)),
        out_shape=jax.ShapeDtypeStruct((m, d), F32),
        compiler_params=_params(("arbitrary",)),
        name="swiglu",
    )(x, g.reshape(1, d), wg, wu, wd)


def _rope_lanes(x, c, sa, sb):
    return x * c + pltpu.roll(x, LANES - QK_ROPE // 2, 1) * sa + pltpu.roll(x, QK_ROPE // 2, 1) * sb


def _mla_project_kernel(x_ref, g_ref, win_ref, gq_ref, gkv_ref, wq_ref, wk_ref, wkr_ref, wv_ref,
                        c_ref, sa_ref, sb_ref, q_ref, k_ref, v_ref, *, scale):
    h = _rms(x_ref[...], g_ref[...]).astype(BF16)
    a = jnp.dot(h, win_ref[...], preferred_element_type=F32)
    c_q = _rms(a[:, :Q_LORA], gq_ref[...]).astype(BF16)
    c_kv = _rms(a[:, Q_LORA:Q_LORA + KV_LORA], gkv_ref[...]).astype(BF16)
    k_rope = a[:, Q_LORA + KV_LORA:].astype(BF16)
    q = jnp.dot(c_q, wq_ref[...], preferred_element_type=F32)
    k = jnp.dot(c_kv, wk_ref[...], preferred_element_type=F32)
    v = jnp.dot(c_kv, wv_ref[...], preferred_element_type=F32)
    c, sa, sb = c_ref[...], sa_ref[...], sb_ref[...]
    ones_lane = lax.broadcasted_iota(jnp.int32, (1, 2 * LANES), 1) % LANES == V_DIM
    shared = _rope_lanes(jnp.dot(k_rope, wkr_ref[...], preferred_element_type=F32), c, sa, sb)
    for p in range(MLA_HEADS // 2):
        qs, ks = [], []
        for half in range(2):
            cols = slice((2 * p + half) * LANES, (2 * p + half + 1) * LANES)
            qs.append((_rope_lanes(q[:, cols], c, sa, sb) * scale).astype(BF16))
            ks.append((k[:, cols] + shared).astype(BF16))
        q_ref[p] = jnp.concatenate(qs, axis=1)
        k_ref[p] = jnp.concatenate(ks, axis=1)
        v_ref[p] = jnp.where(ones_lane, 1.0, v[:, 2 * p * LANES:2 * (p + 1) * LANES]).astype(BF16)


def _mla_project(x, g, w_in, gq, gkv, wq_pad, wk_pad, wkr, wv, tables, *, tm):
    b, s, d = x.shape
    c, sa, sb = tables
    pairs = MLA_HEADS // 2
    scale = (QK_NOPE + QK_ROPE) ** -0.5 * LOG2_E

    def full(arr):
        return pl.BlockSpec(arr.shape, lambda bi, i: (0,) * arr.ndim)

    tab = pl.BlockSpec((None, tm, LANES), lambda bi, i: (bi, i, 0))
    ops = [g.reshape(1, d), w_in, gq.reshape(1, -1), gkv.reshape(1, -1), wq_pad, wk_pad, wkr, wv]
    return pl.pallas_call(
        functools.partial(_mla_project_kernel, scale=scale),
        grid=(b, s // tm),
        in_specs=[pl.BlockSpec((None, tm, d), lambda bi, i: (bi, i, 0))] + [full(o) for o in ops]
                 + [tab, tab, tab],
        out_specs=[pl.BlockSpec((None, pairs, tm, 2 * LANES), lambda bi, i: (bi, 0, i, 0)),
                   pl.BlockSpec((None, pairs, tm, 2 * LANES), lambda bi, i: (bi, 0, i, 0)),
                   pl.BlockSpec((None, pairs, tm, 2 * LANES), lambda bi, i: (bi, 0, i, 0))],
        out_shape=[jax.ShapeDtypeStruct((b, pairs, s, 2 * LANES), BF16)] * 3,
        compiler_params=_params(("arbitrary", "arbitrary")),
        name="mla_project",
    )(x, *ops, c, sa, sb)


def _flash_kernel(q_ref, k_ref, v_ref, o_ref, s_scr, m_scr, acc_scr, *, tq, tk):
    qi = pl.program_id(2)
    m_scr[...] = jnp.full(m_scr.shape, MASK_VALUE, F32)
    acc_scr[...] = jnp.zeros(acc_scr.shape, F32)
    nt = (((1,), (1,)), ((), ()))

    def key_rows(kj):
        return pl.ds(pl.multiple_of(kj * tk, tk), tk)

    def head_cols(half):
        return slice(half * LANES, (half + 1) * LANES)

    def scores(kj, slot):
        ks = k_ref[key_rows(kj), :]
        for half in range(2):
            s_scr[slot, half] = lax.dot_general(q_ref[:, head_cols(half)], ks[:, head_cols(half)], nt,
                                                preferred_element_type=F32)

    def update(kj, rows, s_pair, triangular):
        n = rows.stop - rows.start
        vs = v_ref[key_rows(kj), :]
        for half in range(2):
            s = s_pair[half]
            if triangular:
                row = lax.broadcasted_iota(jnp.int32, (n, tk), 0)
                col = lax.broadcasted_iota(jnp.int32, (n, tk), 1)
                s = jnp.where(row >= col, s, MASK_VALUE)
            m_prev = m_scr[half, rows]
            m_new = jnp.maximum(m_prev, jnp.max(s, axis=-1, keepdims=True))
            e = jnp.exp2(s - jnp.tile(m_new, (1, tk // LANES)))
            m_scr[half, rows] = m_new
            pv = jnp.dot(e.astype(BF16), vs[:, head_cols(half)], preferred_element_type=F32)
            acc_scr[half, rows] = acc_scr[half, rows] * jnp.exp2(m_prev - m_new) + pv

    everything = slice(0, tq)
    top, bottom = slice(0, tk), slice(tk, tq)

    def from_scratch(slot, rows):
        return [s_scr[slot, half, rows] for half in range(2)]

    scores(0, 0)

    def body(jj, carry):
        scores(2 * jj + 1, 1)
        update(2 * jj, everything, from_scratch(0, everything), False)
        scores(2 * jj + 2, 0)
        update(2 * jj + 1, everything, from_scratch(1, everything), False)
        return carry

    lax.fori_loop(0, qi, body, 0)
    update(2 * qi, top, from_scratch(0, top), True)
    update(2 * qi, bottom, from_scratch(0, bottom), False)
    ks = k_ref[key_rows(2 * qi + 1), :]
    last = [lax.dot_general(q_ref[bottom, head_cols(half)], ks[:, head_cols(half)], nt,
                            preferred_element_type=F32) for half in range(2)]
    update(2 * qi + 1, bottom, last, True)
    out = [acc_scr[half] / acc_scr[half][:, V_DIM:V_DIM + 1] for half in range(2)]
    low = lax.broadcasted_iota(jnp.int32, (tq, LANES), 1) < V_DIM
    o_ref[...] = jnp.where(low, out[0], pltpu.roll(out[1], V_DIM, 1)).astype(o_ref.dtype)


def _flash_attention(q, k, v, *, tk):
    b, pairs, s, _ = q.shape
    tq = 2 * tk
    return pl.pallas_call(
        functools.partial(_flash_kernel, tq=tq, tk=tk),
        grid=(b, pairs, s // tq),
        in_specs=[pl.BlockSpec((None, None, tq, 2 * LANES), lambda bi, p, i: (bi, p, i, 0)),
                  pl.BlockSpec((None, None, s, 2 * LANES), lambda bi, p, i: (bi, p, 0, 0),
                               pipeline_mode=pl.Buffered(1)),
                  pl.BlockSpec((None, None, s, 2 * LANES), lambda bi, p, i: (bi, p, 0, 0),
                               pipeline_mode=pl.Buffered(1))],
        out_specs=pl.BlockSpec((None, tq, LANES), lambda bi, p, i: (bi, i, p)),
        out_shape=jax.ShapeDtypeStruct((b, s, pairs * LANES), BF16),
        scratch_shapes=[pltpu.VMEM((2, 2, tq, tk), F32),
                        pltpu.VMEM((2, tq, LANES), F32), pltpu.VMEM((2, tq, LANES), F32)],
        compiler_params=_params(("arbitrary", "arbitrary", "arbitrary")),
        name="flash_attention",
    )(q, k, v)


def _route_kernel(x_ref, g_ref, wr_ref, meta_ref, cnt_ref, carry):
    i = pl.program_id(0)
    t = x_ref.shape[0]

    @pl.when(i == 0)
    def _():
        carry[...] = jnp.zeros(carry.shape, F32)

    h = _rms(x_ref[...], g_ref[...])
    logits = jnp.dot(h, wr_ref[...], preferred_element_type=F32, precision=lax.Precision.HIGHEST)
    lane = lax.broadcasted_iota(jnp.int32, (t, LANES), 1)
    lg = jnp.where(lane < N_EXPERTS, logits, -jnp.inf)
    m1 = jnp.max(lg, axis=-1, keepdims=True)
    i1 = jnp.min(jnp.where(lg == m1, lane, LANES), axis=-1, keepdims=True)
    lg2 = jnp.where(lane == i1, -jnp.inf, lg)
    m2 = jnp.max(lg2, axis=-1, keepdims=True)
    i2 = jnp.min(jnp.where(lg2 == m2, lane, LANES), axis=-1, keepdims=True)
    e2 = jnp.exp(m2 - m1)
    w1 = 1.0 / (1.0 + e2)
    w2 = e2 / (1.0 + e2)
    hot1 = lane == i1
    hot2 = lane == i2
    onehot = (hot1 | hot2).astype(BF16)
    r = lax.broadcasted_iota(jnp.int32, (t, t), 0)
    c = lax.broadcasted_iota(jnp.int32, (t, t), 1)
    lower = (c < r).astype(BF16)
    before = jnp.dot(lower, onehot, preferred_element_type=F32) + carry[...]
    rank1 = jnp.sum(jnp.where(hot1, before, 0.0), axis=-1, keepdims=True)
    rank2 = jnp.sum(jnp.where(hot2, before, 0.0), axis=-1, keepdims=True)
    carry[...] += jnp.sum(onehot.astype(F32), axis=0, keepdims=True)
    meta = jnp.zeros((t, LANES), F32)
    for k, val in enumerate((i1.astype(F32), i2.astype(F32), rank1, rank2, w1, w2)):
        meta = jnp.where(lane == k, val, meta)
    meta_ref[...] = meta
    cnt_ref[...] = carry[...]


def _route(x, g, w_router, *, tm):
    m, d = x.shape
    wr = jnp.zeros((d, LANES), F32).at[:, :N_EXPERTS].set(w_router)
    return pl.pallas_call(
        _route_kernel,
        grid=(m // tm,),
        in_specs=[pl.BlockSpec((tm, d), lambda i: (i, 0)),
                  pl.BlockSpec((1, d), lambda i: (0, 0)),
                  pl.BlockSpec((d, LANES), lambda i: (0, 0))],
        out_specs=[pl.BlockSpec((tm, LANES), lambda i: (i, 0)),
                   pl.BlockSpec((1, LANES), lambda i: (0, 0))],
        out_shape=[jax.ShapeDtypeStruct((m, LANES), F32), jax.ShapeDtypeStruct((1, LANES), F32)],
        scratch_shapes=[pltpu.VMEM((1, LANES), F32)],
        compiler_params=_params(("arbitrary",)),
        name="route",
    )(x, g.reshape(1, d), wr)


def _dispatch_kernel(dest_ref, x_ref, g_ref, init_ref, xs_ref, h_scr, sem):
    del init_ref
    t = x_ref.shape[0]
    h_scr[...] = _rms(x_ref[...], g_ref[...])

    def row_copy(j, slot):
        return pltpu.make_async_copy(h_scr.at[pl.ds(j, 1)],
                                     xs_ref.at[pl.ds(dest_ref[2 * j + slot], 1)], sem)

    def start(j, carry):
        row_copy(j, 0).start(priority=0)
        row_copy(j, 1).start(priority=1)
        return carry

    lax.fori_loop(0, t, start, 0, unroll=8)
    for _ in range(2):
        pltpu.make_async_copy(h_scr, xs_ref.at[pl.ds(0, t)], sem).wait()


def _dispatch(x, g, dest, rows, *, tm):
    m, d = x.shape
    init = jnp.zeros((rows, d), F32)
    return pl.pallas_call(
        _dispatch_kernel,
        grid=(m // tm,),
        in_specs=[pl.BlockSpec((2 * tm,), lambda i: (i,), memory_space=pltpu.SMEM),
                  pl.BlockSpec((tm, d), lambda i: (i, 0)),
                  pl.BlockSpec((1, d), lambda i: (0, 0)),
                  pl.BlockSpec(memory_space=pl.ANY)],
        out_specs=pl.BlockSpec(memory_space=pl.ANY),
        out_shape=jax.ShapeDtypeStruct((rows, d), F32),
        scratch_shapes=[pltpu.VMEM((tm, d), F32), pltpu.SemaphoreType.DMA(())],
        input_output_aliases={3: 0},
        compiler_params=_params(("arbitrary",)),
        name="dispatch",
    )(dest, x, g.reshape(1, d), init)


def _grouped_kernel(te_ref, tv_ref, xs_ref, wg_ref, wu_ref, wd_ref, y_ref):
    i = pl.program_id(0)

    @pl.when(tv_ref[i] > 0)
    def _():
        h = xs_ref[...].astype(BF16)
        gate = jnp.dot(h, wg_ref[...], preferred_element_type=F32)
        up = jnp.dot(h, wu_ref[...], preferred_element_type=F32)
        act = (gate * jax.nn.sigmoid(gate) * up).astype(BF16)
        y_ref[...] = jnp.dot(act, wd_ref[...], preferred_element_type=F32)

    @pl.when(tv_ref[i] == 0)
    def _():
        y_ref[...] = jnp.zeros(y_ref.shape, F32)


def _grouped_swiglu(xs, tile_expert, tile_valid, wg, wu, wd, *, tm):
    rows, d = xs.shape
    ff = wg.shape[2]
    resident = pl.Buffered(1)
    grid_spec = pltpu.PrefetchScalarGridSpec(
        num_scalar_prefetch=2,
        grid=(rows // tm,),
        in_specs=[pl.BlockSpec((tm, d), lambda i, te, tv: (i, 0)),
                  pl.BlockSpec((None, d, ff), lambda i, te, tv: (te[i], 0, 0), pipeline_mode=resident),
                  pl.BlockSpec((None, d, ff), lambda i, te, tv: (te[i], 0, 0), pipeline_mode=resident),
                  pl.BlockSpec((None, ff, d), lambda i, te, tv: (te[i], 0, 0), pipeline_mode=resident)],
        out_specs=pl.BlockSpec((tm, d), lambda i, te, tv: (i, 0)),
    )
    return pl.pallas_call(
        _grouped_kernel,
        grid_spec=grid_spec,
        out_shape=jax.ShapeDtypeStruct((rows, d), F32),
        compiler_params=_params(("arbitrary",), vmem=GROUPED_VMEM_LIMIT),
        name="grouped_swiglu",
    )(tile_expert, tile_valid, xs, wg, wu, wd)


def _combine_kernel(dest_ref, x_ref, meta_ref, g_ref, y_ref, o_ref, y0_scr, y1_scr, sem):
    t = x_ref.shape[0]

    def row_copy(j, slot):
        dst = y0_scr if slot == 0 else y1_scr
        return pltpu.make_async_copy(y_ref.at[pl.ds(dest_ref[2 * j + slot], 1)],
                                     dst.at[pl.ds(j, 1)], sem)

    def start(j, carry):
        row_copy(j, 0).start(priority=0)
        row_copy(j, 1).start(priority=1)
        return carry

    lax.fori_loop(0, t, start, 0, unroll=8)
    for dst in (y0_scr, y1_scr):
        pltpu.make_async_copy(y_ref.at[pl.ds(0, t)], dst, sem).wait()
    meta = meta_ref[...]
    x = x_ref[...] + (meta[:, 4:5] * y0_scr[...] + meta[:, 5:6] * y1_scr[...])
    o_ref[...] = _rms(x, g_ref[...])


def _combine(x, meta, dest, y, g, *, tm):
    m, d = x.shape
    return pl.pallas_call(
        _combine_kernel,
        grid=(m // tm,),
        in_specs=[pl.BlockSpec((2 * tm,), lambda i: (i,), memory_space=pltpu.SMEM),
                  pl.BlockSpec((tm, d), lambda i: (i, 0)),
                  pl.BlockSpec((tm, LANES), lambda i: (i, 0)),
                  pl.BlockSpec((1, d), lambda i: (0, 0)),
                  pl.BlockSpec(memory_space=pl.ANY)],
        out_specs=pl.BlockSpec((tm, d), lambda i: (i, 0)),
        out_shape=jax.ShapeDtypeStruct((m, d), F32),
        scratch_shapes=[pltpu.VMEM((tm, d), F32), pltpu.VMEM((tm, d), F32),
                        pltpu.SemaphoreType.DMA(())],
        compiler_params=_params(("arbitrary",)),
        name="combine",
    )(dest, x, meta, g.reshape(1, d), y)


def _moe_layer(x, g, w_router, wg, wu, wd, final_g, *, tm):
    m, d = x.shape
    meta, counts = _route(x, g, w_router, tm=512)
    cnt = counts[0, :N_EXPERTS].astype(jnp.int32)
    tiles = (cnt + tm - 1) // tm
    tile_end = jnp.cumsum(tiles)
    offset = (tile_end - tiles) * tm
    ids = meta[:, 0:2].astype(jnp.int32)
    expert_ids = jnp.arange(N_EXPERTS, dtype=jnp.int32)
    base = jnp.sum(jnp.where(ids[:, :, None] == expert_ids, offset, 0), axis=-1)
    dest = (base + meta[:, 2:4].astype(jnp.int32)).reshape(2 * m)
    n_tiles = (2 * m) // tm + N_EXPERTS
    tile_id = jnp.arange(n_tiles, dtype=jnp.int32)
    tile_expert = jnp.sum(tile_id[:, None] >= tile_end[None, :], axis=1).astype(jnp.int32)
    tile_valid = (tile_expert < N_EXPERTS).astype(jnp.int32)
    tile_expert = jnp.minimum(tile_expert, N_EXPERTS - 1)
    xs = _dispatch(x, g, dest, n_tiles * tm, tm=256)
    y = _grouped_swiglu(xs, tile_expert, tile_valid, wg, wu, wd, tm=tm)
    return _combine(x, meta, dest, y, final_g, tm=256)


def _mla_weights(w_q_up, w_kv_up):
    hd = QK_NOPE + QK_ROPE
    wq = w_q_up.reshape(Q_LORA, MLA_HEADS, hd)
    wq_pad = jnp.pad(wq, ((0, 0), (0, 0), (0, LANES - hd))).reshape(Q_LORA, MLA_HEADS * LANES)
    wkv = w_kv_up.reshape(KV_LORA, MLA_HEADS, QK_NOPE + V_DIM)
    wk_pad = jnp.pad(wkv[:, :, :QK_NOPE], ((0, 0), (0, 0), (0, LANES - QK_NOPE)))
    wk_pad = wk_pad.reshape(KV_LORA, MLA_HEADS * LANES)
    wv = jnp.pad(wkv[:, :, QK_NOPE:], ((0, 0), (0, 0), (0, LANES - V_DIM))).reshape(KV_LORA, MLA_HEADS * LANES)
    place = jnp.zeros((QK_ROPE, LANES), F32).at[jnp.arange(QK_ROPE), QK_NOPE + jnp.arange(QK_ROPE)].set(1.0)
    return wq_pad.astype(BF16), wk_pad.astype(BF16), place.astype(BF16), wv.astype(BF16)


def _rope_lane_tables(positions):
    half = QK_ROPE // 2
    inv = ROPE_THETA ** (-jnp.arange(0, QK_ROPE, 2, dtype=F32) / QK_ROPE)
    ang = positions.astype(F32)[..., None] * inv
    cos, sin = jnp.cos(ang), jnp.sin(ang)
    shape = positions.shape
    ones = jnp.ones(shape + (QK_NOPE,), F32)
    zeros_n = jnp.zeros(shape + (QK_NOPE,), F32)
    zeros_h = jnp.zeros(shape + (half,), F32)
    pad = jnp.zeros(shape + (LANES - QK_NOPE - QK_ROPE,), F32)
    c = jnp.concatenate([ones, cos, cos, pad], axis=-1)
    sa = jnp.concatenate([zeros_n, -sin, zeros_h, pad], axis=-1)
    sb = jnp.concatenate([zeros_n, zeros_h, sin, pad], axis=-1)
    return c, sa, sb


def kernel(x, positions, mix_norm_g, ffn_norm_g, a_w_qkv, a_w_out, b_w_in, b_q_norm_g, b_kv_norm_g,
           b_w_q_up, b_w_kv_up, b_w_out, ffn_w_gate, ffn_w_up, ffn_w_down, moe_w_router, moe_w_gate,
           moe_w_up, moe_w_down, final_norm_g):
    b, s, d = x.shape
    m = b * s
    xf = x.reshape(m, d)

    group_cols = 3 * DSA_WIDTH
    q_scale = jnp.where(jnp.arange(a_w_qkv.shape[2]) % group_cols < DSA_WIDTH, DSA_HEAD_DIM ** -0.5, 1.0)
    w_qkv = (a_w_qkv[0] * q_scale).astype(BF16)
    state = None
    for group, (_, dil) in enumerate(DIL_PAIRS):
        w_group = w_qkv[:, group * group_cols:(group + 1) * group_cols]
        qkv_v = _qkv_project(xf, mix_norm_g[0], w_group, dil=dil, tm=1024 if dil == 1 else 512)
        qkv_v = qkv_v.reshape(b, s // dil, dil * group_cols)
        state = _merge_group(qkv_v, positions, state, group=group, qb=4)
    x1 = _unpermute_project(xf, state, a_w_out[0].astype(BF16), dil=DIL_PAIRS[-1][1], tm=512)
    x2 = _swiglu(x1, ffn_norm_g[0], ffn_w_gate[0].astype(BF16), ffn_w_up[0].astype(BF16),
                 ffn_w_down[0].astype(BF16), tm=512)

    wq_pad, wk_pad, wkr, wv = _mla_weights(b_w_q_up[0], b_w_kv_up[0])
    q, k, v = _mla_project(x2.reshape(b, s, d), mix_norm_g[1], b_w_in[0].astype(BF16), b_q_norm_g[0],
                           b_kv_norm_g[0], wq_pad, wk_pad, wkr, wv, _rope_lane_tables(positions), tm=512)
    attn = _flash_attention(q, k, v, tk=512)
    x3 = _matmul_residual(x2, attn.reshape(m, MLA_HEADS * V_DIM), b_w_out[0].astype(BF16), tm=1024)
    out = _moe_layer(x3, ffn_norm_g[1], moe_w_router[0], moe_w_gate[0].astype(BF16),
                     moe_w_up[0].astype(BF16), moe_w_down[0].astype(BF16), final_norm_g,
                     tm=512)
    return out.reshape(b, s, d)
```

```python
import functools

import jax
import jax.numpy as jnp
from jax import lax
from jax.experimental import pallas as pl
from jax.experimental.pallas import tpu as pltpu

F32 = jnp.float32
BF16 = jnp.bfloat16

RMS_EPS = 1e-6
BLOCK = 128
LANES = 128
DIL_PAIRS = ((128, 1), (512, 4), (2048, 16))
DSA_HEADS = 8
DSA_HEAD_DIM = 64
DSA_WIDTH = DSA_HEADS * DSA_HEAD_DIM
MLA_HEADS = 16
QK_NOPE = 64
QK_ROPE = 32
V_DIM = 64
Q_LORA = 384
KV_LORA = 256
ROPE_THETA = 10000.0
N_EXPERTS = 8
MASK_VALUE = -1e30
LOG2_E = 1.4426950408889634
VMEM_LIMIT = 56 << 20
GROUPED_VMEM_LIMIT = 60 << 20


def _params(semantics, vmem=VMEM_LIMIT):
    return pltpu.CompilerParams(dimension_semantics=semantics, vmem_limit_bytes=vmem)


def _rms(x, g):
    ms = jnp.mean(x * x, axis=-1, keepdims=True)
    return x * lax.rsqrt(ms + RMS_EPS) * g


def _residue_slot(dil):
    if dil == 16:
        return tuple(4 * (r % 4) + r // 4 for r in range(dil))
    return tuple(range(dil))


def _deinterleave(tm, dil):
    n = tm // dil
    out_row = jnp.arange(tm, dtype=jnp.int32)
    source = (out_row % n) * dil + out_row // n
    return (source[:, None] == out_row[None, :]).astype(BF16)


def _qkv_kernel(x_ref, g_ref, perm_ref, w_ref, o_ref, *, dil):
    tm = x_ref.shape[0]
    n = tm // dil
    width = w_ref.shape[1]
    h = _rms(x_ref[...], g_ref[...]).astype(BF16)
    if dil > 1:
        h = jnp.dot(perm_ref[...], h, preferred_element_type=F32).astype(BF16)
    res = jnp.dot(h, w_ref[...], preferred_element_type=F32)
    for r, slot in enumerate(_residue_slot(dil)):
        o_ref[:, slot * width:(slot + 1) * width] = res[r * n:(r + 1) * n, :].astype(o_ref.dtype)


def _qkv_project(x, g, w, *, dil, tm):
    m, d = x.shape
    width = w.shape[1]
    n = tm // dil
    perm = _deinterleave(tm, dil)
    return pl.pallas_call(
        functools.partial(_qkv_kernel, dil=dil),
        grid=(m // tm,),
        in_specs=[pl.BlockSpec((tm, d), lambda i: (i, 0)),
                  pl.BlockSpec((1, d), lambda i: (0, 0)),
                  pl.BlockSpec((tm, tm), lambda i: (0, 0)),
                  pl.BlockSpec((d, width), lambda i: (0, 0))],
        out_specs=pl.BlockSpec((n, dil * width), lambda i: (i, 0)),
        out_shape=jax.ShapeDtypeStruct((m // dil, dil * width), BF16),
        compiler_params=_params(("arbitrary",)),
        name=f"qkv_project_d{dil}",
    )(x, g.reshape(1, d), perm, w)


def _matmul_residual_kernel(x_ref, a_ref, w_ref, o_ref):
    o_ref[...] = x_ref[...] + jnp.dot(a_ref[...], w_ref[...], preferred_element_type=F32)


def _matmul_residual(x, a, w, *, tm):
    m, d = x.shape
    k = a.shape[1]
    return pl.pallas_call(
        _matmul_residual_kernel,
        grid=(m // tm,),
        in_specs=[pl.BlockSpec((tm, d), lambda i: (i, 0)),
                  pl.BlockSpec((tm, k), lambda i: (i, 0)),
                  pl.BlockSpec((k, d), lambda i: (0, 0))],
        out_specs=pl.BlockSpec((tm, d), lambda i: (i, 0)),
        out_shape=jax.ShapeDtypeStruct((m, d), F32),
        compiler_params=_params(("arbitrary",)),
        name="matmul_residual",
    )(x, a, w)


def _merge_kernel(*refs, qb, span, has_prev, emit_lse):
    q_ref, k_ref, v_ref, prow_ref = refs[:4]
    pos = 4
    if has_prev:
        oprev_ref, lprev_ref = refs[pos:pos + 2]
        pos += 2
    if emit_lse:
        perm_ref, o_ref, lse_ref = refs[pos:pos + 3]
        pos += 3
    else:
        o_ref = refs[pos]
        pos += 1
    kbuf, vbuf, pbuf = refs[pos:pos + 3]
    if emit_lse:
        o_scr, lse_scr = refs[pos + 3:pos + 5]

    n = pl.program_id(2)
    tq = qb * BLOCK

    @pl.when(n == 0)
    def _():
        kbuf[0:BLOCK, :] = jnp.zeros((BLOCK, DSA_WIDTH), BF16)
        vbuf[0:BLOCK, :] = jnp.zeros((BLOCK, DSA_WIDTH), BF16)
        pbuf[:, 0:BLOCK] = jnp.zeros((1, BLOCK), jnp.int32)

    kbuf[BLOCK:, :] = k_ref[...]
    vbuf[BLOCK:, :] = v_ref[...]
    pbuf[:, BLOCK:] = prow_ref[...]

    lane = lax.broadcasted_iota(jnp.int32, (BLOCK, LANES), 1)
    sub = lax.broadcasted_iota(jnp.int32, (BLOCK, LANES), 0)
    low = lane < DSA_HEAD_DIM
    diag = lane == sub
    row = lax.broadcasted_iota(jnp.int32, (BLOCK, 2 * BLOCK), 0)
    col = lax.broadcasted_iota(jnp.int32, (BLOCK, 2 * BLOCK), 1)
    rel = row + BLOCK - col
    band = (rel >= 0) & (rel <= span)
    first_valid = band & ((col >= BLOCK) | (n > 0))
    spread = (lax.broadcasted_iota(jnp.int32, (LANES, DSA_WIDTH), 0)
              == lax.broadcasted_iota(jnp.int32, (LANES, DSA_WIDTH), 1) // DSA_HEAD_DIM).astype(BF16)

    def per_head(c):
        hi = c.astype(BF16)
        lo = (c - hi.astype(F32)).astype(BF16)
        return (jnp.dot(hi, spread, preferred_element_type=F32)
                + jnp.dot(lo, spread, preferred_element_type=F32))

    def lane_to_column(v):
        return jnp.sum(jnp.where(diag, v, 0.0), axis=1, keepdims=True)

    for i in range(qb):
        rows = slice(i * BLOCK, (i + 1) * BLOCK)
        keys = slice(i * BLOCK, (i + 2) * BLOCK)
        kk = kbuf[keys, :]
        vv = vbuf[keys, :]
        pk = pbuf[:, keys]
        pq_row = pbuf[:, (i + 1) * BLOCK:(i + 2) * BLOCK]
        pq = ((lane_to_column((pq_row >> 12).astype(F32)).astype(jnp.int32) << 12)
              | lane_to_column((pq_row & 4095).astype(F32)).astype(jnp.int32))
        dist = jnp.abs(pq - pk).astype(F32)
        neg_dist = jnp.where(first_valid if i == 0 else band, -dist, MASK_VALUE)
        q = q_ref[rows, :]
        m_tile = jnp.zeros((BLOCK, LANES), F32)
        l_tile = jnp.ones((BLOCK, LANES), F32)
        pv_pairs = []
        for p in range(DSA_HEADS // 2):
            cols = slice(p * LANES, (p + 1) * LANES)
            q2, k2, v2 = q[:, cols], kk[:, cols], vv[:, cols]
            pvs = []
            for half in range(2):
                h = 2 * p + half
                slope = 2.0 ** (-8.0 * (h + 1) / DSA_HEADS)
                qm = jnp.where(low if half == 0 else ~low, q2, jnp.zeros_like(q2))
                s = lax.dot_general(qm, k2, (((1,), (1,)), ((), ())), preferred_element_type=F32)
                s = s + slope * neg_dist
                m = jnp.max(s, axis=-1, keepdims=True)
                e = jnp.exp(s - m)
                m_tile = jnp.where(lane == h, m, m_tile)
                l_tile = jnp.where(lane == h, jnp.sum(e, axis=-1, keepdims=True), l_tile)
                pvs.append(jnp.dot(e.astype(BF16), v2, preferred_element_type=F32))
            pv_pairs.append(jnp.where(low, pvs[0], pvs[1]))
        pv_all = jnp.concatenate(pv_pairs, axis=1)
        lse = m_tile + jnp.log(l_tile)
        if has_prev:
            lp = lprev_ref[rows, :]
            mx = jnp.maximum(lp, lse)
            wa = jnp.exp(lp - mx)
            wb = jnp.exp(lse - mx)
            den = wa + wb
            c_new = wb / (den * l_tile)
            lse = mx + jnp.log(den)
            out = pv_all * per_head(c_new) + oprev_ref[rows, :].astype(F32) * per_head(wa / den)
        else:
            out = pv_all * per_head(1.0 / l_tile)
        if emit_lse:
            o_scr[rows, :] = out.astype(BF16)
            lse_scr[rows, :] = lse
        else:
            o_ref[rows, :] = out.astype(o_ref.dtype)

    if emit_lse:
        part = tq // 4
        o_perm = jnp.dot(perm_ref[...], o_scr[...], preferred_element_type=F32)
        for r in range(4):
            o_ref[:, r * DSA_WIDTH:(r + 1) * DSA_WIDTH] = o_perm[r * part:(r + 1) * part, :].astype(o_ref.dtype)
            lse_ref[:, r * LANES:(r + 1) * LANES] = lse_scr[pl.ds(r, part, stride=4), :]

    tail = slice(qb * BLOCK, (qb + 1) * BLOCK)
    kbuf[0:BLOCK, :] = kbuf[tail, :]
    vbuf[0:BLOCK, :] = vbuf[tail, :]
    pbuf[:, 0:BLOCK] = pbuf[:, tail]


def _merge_group(qkv_v, positions, prev, *, group, qb):
    b, s = positions.shape
    window, dil = DIL_PAIRS[group]
    last = group == len(DIL_PAIRS) - 1
    span = window // dil
    sub = s // dil
    tq = qb * BLOCK
    nb = sub // tq
    residue_of_slot = sorted(range(dil), key=_residue_slot(dil).__getitem__)
    prow = positions.reshape(b, sub, dil).transpose(0, 2, 1)[:, jnp.asarray(residue_of_slot), :]
    prow = prow.reshape(b, dil, 1, sub)

    def qkv_spec(which):
        return pl.BlockSpec((None, tq, DSA_WIDTH), lambda bi, r, n: (bi, n, r * 3 + which))

    in_specs = [qkv_spec(0), qkv_spec(1), qkv_spec(2),
                pl.BlockSpec((None, None, 1, tq), lambda bi, r, n: (bi, r, 0, n))]
    args = [qkv_v, qkv_v, qkv_v, prow]
    if prev is not None:
        in_specs += [pl.BlockSpec((None, tq, DSA_WIDTH), lambda bi, r, n: (bi, n, r)),
                     pl.BlockSpec((None, tq, LANES), lambda bi, r, n: (bi, n, r))]
        args += list(prev)
    scratch = [pltpu.VMEM((tq + BLOCK, DSA_WIDTH), BF16), pltpu.VMEM((tq + BLOCK, DSA_WIDTH), BF16),
               pltpu.VMEM((1, tq + BLOCK), jnp.int32)]
    if last:
        out_specs = [pl.BlockSpec((None, tq, DSA_WIDTH), lambda bi, r, n: (bi, n, r))]
        out_shape = [jax.ShapeDtypeStruct((b, sub, dil * DSA_WIDTH), BF16)]
    else:
        in_specs.append(pl.BlockSpec((tq, tq), lambda bi, r, n: (0, 0)))
        args.append(_deinterleave(tq, 4))
        out_specs = [pl.BlockSpec((None, tq // 4, 4 * DSA_WIDTH), lambda bi, r, n: (bi, n, r)),
                     pl.BlockSpec((None, tq // 4, 4 * LANES), lambda bi, r, n: (bi, n, r))]
        out_shape = [jax.ShapeDtypeStruct((b, sub // 4, 4 * dil * DSA_WIDTH), BF16),
                     jax.ShapeDtypeStruct((b, sub // 4, 4 * dil * LANES), F32)]
        scratch += [pltpu.VMEM((tq, DSA_WIDTH), BF16), pltpu.VMEM((tq, LANES), F32)]
    outs = pl.pallas_call(
        functools.partial(_merge_kernel, qb=qb, span=span, has_prev=prev is not None, emit_lse=not last),
        grid=(b, dil, nb),
        in_specs=in_specs,
        out_specs=out_specs,
        out_shape=out_shape,
        scratch_shapes=scratch,
        compiler_params=_params(("arbitrary", "arbitrary", "arbitrary")),
        name=f"dilated_group{group}",
    )(*args)
    return outs[0] if last else tuple(outs)


def _unpermute_project_kernel(x_ref, a_ref, perm_ref, w_ref, o_ref, *, dil):
    width = w_ref.shape[0]
    stacked = jnp.concatenate([a_ref[:, slot * width:(slot + 1) * width] for slot in _residue_slot(dil)],
                              axis=0)
    tokens = jnp.dot(perm_ref[...], stacked, preferred_element_type=F32).astype(BF16)
    o_ref[...] = x_ref[...] + jnp.dot(tokens, w_ref[...], preferred_element_type=F32)


def _unpermute_project(x, a, w, *, dil, tm):
    m, d = x.shape
    width = w.shape[0]
    return pl.pallas_call(
        functools.partial(_unpermute_project_kernel, dil=dil),
        grid=(m // tm,),
        in_specs=[pl.BlockSpec((tm, d), lambda i: (i, 0)),
                  pl.BlockSpec((tm // dil, dil * width), lambda i: (i, 0)),
                  pl.BlockSpec((tm, tm), lambda i: (0, 0)),
                  pl.BlockSpec((width, d), lambda i: (0, 0))],
        out_specs=pl.BlockSpec((tm, d), lambda i: (i, 0)),
        out_shape=jax.ShapeDtypeStruct((m, d), F32),
        compiler_params=_params(("arbitrary",)),
        name="unpermute_project",
    )(x, a.reshape(m // dil, dil * width), _deinterleave(tm, dil).T, w)


def _swiglu_kernel(x_ref, g_ref, wg_ref, wu_ref, wd_ref, o_ref):
    x = x_ref[...]
    h = _rms(x, g_ref[...]).astype(BF16)
    gate = jnp.dot(h, wg_ref[...], preferred_element_type=F32)
    up = jnp.dot(h, wu_ref[...], preferred_element_type=F32)
    act = (gate * jax.nn.sigmoid(gate) * up).astype(BF16)
    o_ref[...] = x + jnp.dot(act, wd_ref[...], preferred_element_type=F32)


def _swiglu(x, g, wg, wu, wd, *, tm):
    m, d = x.shape
    ff = wg.shape[1]
    resident = pl.Buffered(1)
    return pl.pallas_call(
        _swiglu_kernel,
        grid=(m // tm,),
        in_specs=[pl.BlockSpec((tm, d), lambda i: (i, 0)),
                  pl.BlockSpec((1, d), lambda i: (0, 0)),
                  pl.BlockSpec((d, ff), lambda i: (0, 0), pipeline_mode=resident),
                  pl.BlockSpec((d, ff), lambda i: (0, 0), pipeline_mode=resident),
                  pl.BlockSpec((ff, d), lambda i: (0, 0), pipeline_mode=resident)],
        out_specs=pl.BlockSpec((tm, d), lambda i: (i, 0)),
        out_shape=jax.ShapeDtypeStruct((m, d), F32),
        compiler_params=_params(("arbitrary",)),
        name="swiglu",
    )(x, g.reshape(1, d), wg, wu, wd)


def _rope_lanes(x, c, sa, sb):
    return x * c + pltpu.roll(x, LANES - QK_ROPE // 2, 1) * sa + pltpu.roll(x, QK_ROPE // 2, 1) * sb


def _mla_project_kernel(x_ref, g_ref, win_ref, gq_ref, gkv_ref, wq_ref, wk_ref, wkr_ref, wv_ref,
                        c_ref, sa_ref, sb_ref, q_ref, k_ref, v_ref, *, scale):
    h = _rms(x_ref[...], g_ref[...]).astype(BF16)
    a = jnp.dot(h, win_ref[...], preferred_element_type=F32)
    c_q = _rms(a[:, :Q_LORA], gq_ref[...]).astype(BF16)
    c_kv = _rms(a[:, Q_LORA:Q_LORA + KV_LORA], gkv_ref[...]).astype(BF16)
    k_rope = a[:, Q_LORA + KV_LORA:].astype(BF16)
    q = jnp.dot(c_q, wq_ref[...], preferred_element_type=F32)
    k = jnp.dot(c_kv, wk_ref[...], preferred_element_type=F32)
    v = jnp.dot(c_kv, wv_ref[...], preferred_element_type=F32)
    c, sa, sb = c_ref[...], sa_ref[...], sb_ref[...]
    ones_lane = lax.broadcasted_iota(jnp.int32, (1, 2 * LANES), 1) % LANES == V_DIM
    shared = _rope_lanes(jnp.dot(k_rope, wkr_ref[...], preferred_element_type=F32), c, sa, sb)
    for p in range(MLA_HEADS // 2):
        qs, ks = [], []
        for half in range(2):
            cols = slice((2 * p + half) * LANES, (2 * p + half + 1) * LANES)
            qs.append((_rope_lanes(q[:, cols], c, sa, sb) * scale).astype(BF16))
            ks.append((k[:, cols] + shared).astype(BF16))
        q_ref[p] = jnp.concatenate(qs, axis=1)
        k_ref[p] = jnp.concatenate(ks, axis=1)
        v_ref[p] = jnp.where(ones_lane, 1.0, v[:, 2 * p * LANES:2 * (p + 1) * LANES]).astype(BF16)


def _mla_project(x, g, w_in, gq, gkv, wq_pad, wk_pad, wkr, wv, tables, *, tm):
    b, s, d = x.shape
    c, sa, sb = tables
    pairs = MLA_HEADS // 2
    scale = (QK_NOPE + QK_ROPE) ** -0.5 * LOG2_E

    def full(arr):
        return pl.BlockSpec(arr.shape, lambda bi, i: (0,) * arr.ndim)

    tab = pl.BlockSpec((None, tm, LANES), lambda bi, i: (bi, i, 0))
    ops = [g.reshape(1, d), w_in, gq.reshape(1, -1), gkv.reshape(1, -1), wq_pad, wk_pad, wkr, wv]
    return pl.pallas_call(
        functools.partial(_mla_project_kernel, scale=scale),
        grid=(b, s // tm),
        in_specs=[pl.BlockSpec((None, tm, d), lambda bi, i: (bi, i, 0))] + [full(o) for o in ops]
                 + [tab, tab, tab],
        out_specs=[pl.BlockSpec((None, pairs, tm, 2 * LANES), lambda bi, i: (bi, 0, i, 0)),
                   pl.BlockSpec((None, pairs, tm, 2 * LANES), lambda bi, i: (bi, 0, i, 0)),
                   pl.BlockSpec((None, pairs, tm, 2 * LANES), lambda bi, i: (bi, 0, i, 0))],
        out_shape=[jax.ShapeDtypeStruct((b, pairs, s, 2 * LANES), BF16)] * 3,
        compiler_params=_params(("arbitrary", "arbitrary")),
        name="mla_project",
    )(x, *ops, c, sa, sb)


def _flash_kernel(q_ref, k_ref, v_ref, o_ref, s_scr, m_scr, acc_scr, *, tq, tk):
    qi = pl.program_id(2)
    m_scr[...] = jnp.full(m_scr.shape, MASK_VALUE, F32)
    acc_scr[...] = jnp.zeros(acc_scr.shape, F32)
    nt = (((1,), (1,)), ((), ()))

    def key_rows(kj):
        return pl.ds(pl.multiple_of(kj * tk, tk), tk)

    def head_cols(half):
        return slice(half * LANES, (half + 1) * LANES)

    def scores(kj, slot):
        ks = k_ref[key_rows(kj), :]
        for half in range(2):
            s_scr[slot, half] = lax.dot_general(q_ref[:, head_cols(half)], ks[:, head_cols(half)], nt,
                                                preferred_element_type=F32)

    def update(kj, rows, s_pair, triangular):
        n = rows.stop - rows.start
        vs = v_ref[key_rows(kj), :]
        for half in range(2):
            s = s_pair[half]
            if triangular:
                row = lax.broadcasted_iota(jnp.int32, (n, tk), 0)
                col = lax.broadcasted_iota(jnp.int32, (n, tk), 1)
                s = jnp.where(row >= col, s, MASK_VALUE)
            m_prev = m_scr[half, rows]
            m_new = jnp.maximum(m_prev, jnp.max(s, axis=-1, keepdims=True))
            e = jnp.exp2(s - jnp.tile(m_new, (1, tk // LANES)))
            m_scr[half, rows] = m_new
            pv = jnp.dot(e.astype(BF16), vs[:, head_cols(half)], preferred_element_type=F32)
            acc_scr[half, rows] = acc_scr[half, rows] * jnp.exp2(m_prev - m_new) + pv

    everything = slice(0, tq)
    top, bottom = slice(0, tk), slice(tk, tq)

    def from_scratch(slot, rows):
        return [s_scr[slot, half, rows] for half in range(2)]

    scores(0, 0)

    def body(jj, carry):
        scores(2 * jj + 1, 1)
        update(2 * jj, everything, from_scratch(0, everything), False)
        scores(2 * jj + 2, 0)
        update(2 * jj + 1, everything, from_scratch(1, everything), False)
        return carry

    lax.fori_loop(0, qi, body, 0)
    update(2 * qi, top, from_scratch(0, top), True)
    update(2 * qi, bottom, from_scratch(0, bottom), False)
    ks = k_ref[key_rows(2 * qi + 1), :]
    last = [lax.dot_general(q_ref[bottom, head_cols(half)], ks[:, head_cols(half)], nt,
                            preferred_element_type=F32) for half in range(2)]
    update(2 * qi + 1, bottom, last, True)
    out = [acc_scr[half] / acc_scr[half][:, V_DIM:V_DIM + 1] for half in range(2)]
    low = lax.broadcasted_iota(jnp.int32, (tq, LANES), 1) < V_DIM
    o_ref[...] = jnp.where(low, out[0], pltpu.roll(out[1], V_DIM, 1)).astype(o_ref.dtype)


def _flash_attention(q, k, v, *, tk):
    b, pairs, s, _ = q.shape
    tq = 2 * tk
    return pl.pallas_call(
        functools.partial(_flash_kernel, tq=tq, tk=tk),
        grid=(b, pairs, s // tq),
        in_specs=[pl.BlockSpec((None, None, tq, 2 * LANES), lambda bi, p, i: (bi, p, i, 0)),
                  pl.BlockSpec((None, None, s, 2 * LANES), lambda bi, p, i: (bi, p, 0, 0),
                               pipeline_mode=pl.Buffered(1)),
                  pl.BlockSpec((None, None, s, 2 * LANES), lambda bi, p, i: (bi, p, 0, 0),
                               pipeline_mode=pl.Buffered(1))],
        out_specs=pl.BlockSpec((None, tq, LANES), lambda bi, p, i: (bi, i, p)),
        out_shape=jax.ShapeDtypeStruct((b, s, pairs * LANES), BF16),
        scratch_shapes=[pltpu.VMEM((2, 2, tq, tk), F32),
                        pltpu.VMEM((2, tq, LANES), F32), pltpu.VMEM((2, tq, LANES), F32)],
        compiler_params=_params(("arbitrary", "arbitrary", "arbitrary")),
        name="flash_attention",
    )(q, k, v)


def _route_kernel(x_ref, g_ref, wr_ref, meta_ref, h_ref, start_ref, cnt_ref, carry):
    i = pl.program_id(0)
    t = x_ref.shape[0]

    @pl.when(i == 0)
    def _():
        carry[...] = jnp.zeros(carry.shape, F32)

    h = _rms(x_ref[...], g_ref[...])
    h_ref[...] = h
    logits = jnp.dot(h, wr_ref[...], preferred_element_type=F32, precision=lax.Precision.HIGHEST)
    lane = lax.broadcasted_iota(jnp.int32, (t, LANES), 1)
    lg = jnp.where(lane < N_EXPERTS, logits, -jnp.inf)
    m1 = jnp.max(lg, axis=-1, keepdims=True)
    i1 = jnp.min(jnp.where(lg == m1, lane, LANES), axis=-1, keepdims=True)
    lg2 = jnp.where(lane == i1, -jnp.inf, lg)
    m2 = jnp.max(lg2, axis=-1, keepdims=True)
    i2 = jnp.min(jnp.where(lg2 == m2, lane, LANES), axis=-1, keepdims=True)
    e2 = jnp.exp(m2 - m1)
    w1 = 1.0 / (1.0 + e2)
    w2 = e2 / (1.0 + e2)
    hot1 = lane == i1
    hot2 = lane == i2
    onehot = (hot1 | hot2).astype(BF16)
    r = lax.broadcasted_iota(jnp.int32, (t, t), 0)
    c = lax.broadcasted_iota(jnp.int32, (t, t), 1)
    lower = (c < r).astype(BF16)
    before = jnp.dot(lower, onehot, preferred_element_type=F32) + carry[...]
    start_ref[...] = carry[...]
    rank1 = jnp.sum(jnp.where(hot1, before, 0.0), axis=-1, keepdims=True)
    rank2 = jnp.sum(jnp.where(hot2, before, 0.0), axis=-1, keepdims=True)
    carry[...] += jnp.sum(onehot.astype(F32), axis=0, keepdims=True)
    meta = jnp.zeros((t, LANES), F32)
    for k, val in enumerate((i1.astype(F32), i2.astype(F32), rank1, rank2, w1, w2)):
        meta = jnp.where(lane == k, val, meta)
    meta_ref[...] = meta
    cnt_ref[...] = carry[...]


def _route(x, g, w_router, *, tm):
    m, d = x.shape
    wr = jnp.zeros((d, LANES), F32).at[:, :N_EXPERTS].set(w_router)
    return pl.pallas_call(
        _route_kernel,
        grid=(m // tm,),
        in_specs=[pl.BlockSpec((tm, d), lambda i: (i, 0)),
                  pl.BlockSpec((1, d), lambda i: (0, 0)),
                  pl.BlockSpec((d, LANES), lambda i: (0, 0))],
        out_specs=[pl.BlockSpec((tm, LANES), lambda i: (i, 0)),
                   pl.BlockSpec((tm, d), lambda i: (i, 0)),
                   pl.BlockSpec((None, 1, LANES), lambda i: (i, 0, 0)),
                   pl.BlockSpec((1, LANES), lambda i: (0, 0))],
        out_shape=[jax.ShapeDtypeStruct((m, LANES), F32),
                   jax.ShapeDtypeStruct((m, d), F32),
                   jax.ShapeDtypeStruct((m // tm, 1, LANES), F32),
                   jax.ShapeDtypeStruct((1, LANES), F32)],
        scratch_shapes=[pltpu.VMEM((1, LANES), F32)],
        compiler_params=_params(("arbitrary",)),
        name="route",
    )(x, g.reshape(1, d), wr)


def _invert_kernel(start_ref, meta_ref, init_ref, list_ref, obuf, sem, *, cap):
    del init_ref
    s = pl.program_id(0)
    t = meta_ref.shape[0]
    meta = meta_ref[...]
    second_expert = meta[:, 1:2]
    by_lane = meta.T
    i1, i2, r1, r2 = by_lane[0:1], by_lane[1:2], by_lane[2:3], by_lane[3:4]
    token = lax.broadcasted_iota(jnp.int32, (t, 1), 0) + s * t
    lane = lax.broadcasted_iota(jnp.int32, (t, LANES), 1)
    place = lax.broadcasted_iota(jnp.int32, (t, t), 0).astype(F32)
    copies = []
    for e in range(N_EXPERTS):
        first, second = i1 == e, i2 == e
        start = start_ref[s * N_EXPERTS + e]
        local = jnp.where(first, r1, r2) - start.astype(F32)
        onehot = ((local == place) & (first | second)).astype(BF16)
        code = 2 * token + (second_expert == e).astype(jnp.int32) + 1
        pieces = jnp.where(lane == 0, (code >> 8).astype(F32),
                           jnp.where(lane == 1, (code & 255).astype(F32), 0.0)).astype(BF16)
        obuf[e] = jnp.dot(onehot, pieces, preferred_element_type=F32)
        copy = pltpu.make_async_copy(obuf.at[e], list_ref.at[pl.ds(e * cap + start, t)], sem)
        copy.start()
        copies.append(copy)
    for copy in copies:
        copy.wait()


def _invert(meta, starts, *, tm, cap):
    m = meta.shape[0]
    rows = N_EXPERTS * cap
    grid_spec = pltpu.PrefetchScalarGridSpec(
        num_scalar_prefetch=1,
        grid=(m // tm,),
        in_specs=[pl.BlockSpec((tm, LANES), lambda i, st: (i, 0)),
                  pl.BlockSpec(memory_space=pl.ANY)],
        out_specs=pl.BlockSpec(memory_space=pl.ANY),
        scratch_shapes=[pltpu.VMEM((N_EXPERTS, tm, LANES), F32), pltpu.SemaphoreType.DMA(())],
    )
    return pl.pallas_call(
        functools.partial(_invert_kernel, cap=cap),
        grid_spec=grid_spec,
        out_shape=jax.ShapeDtypeStruct((rows, LANES), F32),
        input_output_aliases={2: 0},
        compiler_params=_params(("arbitrary",)),
        name="invert",
    )(starts, meta, jnp.zeros((rows, LANES), F32))


def _grouped_kernel(te_ref, tv_ref, tb_ref, tp_ref, cur_ref, nxt_ref, dprev_ref, x_hbm, wg_ref, wu_ref, wd_ref,
                    y_hbm, xbuf, hbuf, ybuf, gsem, ssem):
    del te_ref, tb_ref, tp_ref
    i = pl.program_id(0)
    last = pl.num_programs(0) - 1
    tm = xbuf.shape[1]
    slot = i % 2
    other = 1 - slot

    def gather_row(ids_ref, j, buf):
        return pltpu.make_async_copy(x_hbm.at[pl.ds(ids_ref[j], 1)], xbuf.at[buf, pl.ds(j, 1)], gsem.at[buf])

    def scatter_row(j, buf):
        return pltpu.make_async_copy(ybuf.at[buf, pl.ds(j, 1)], y_hbm.at[pl.ds(dprev_ref[j], 1)], ssem.at[buf])

    def gather_looped(ids_ref, buf):
        def body(j, carry):
            gather_row(ids_ref, j, buf).start()
            return carry
        lax.fori_loop(0, tm, body, 0, unroll=8)

    def scatter_looped(buf):
        def body(j, carry):
            scatter_row(j, buf).start()
            return carry
        lax.fori_loop(0, tm, body, 0, unroll=8)

    @pl.when(i == 0)
    def _():
        ybuf[...] = jnp.zeros(ybuf.shape, F32)
        gather_looped(cur_ref, 0)

    pltpu.make_async_copy(x_hbm.at[pl.ds(0, tm)], xbuf.at[slot], gsem.at[slot]).wait()

    @pl.when(i >= 1)
    def _():
        pltpu.make_async_copy(ybuf.at[slot], y_hbm.at[pl.ds(0, tm)], ssem.at[slot]).wait()

    @pl.when(tv_ref[i] > 0)
    def _():
        hbuf[...] = xbuf[slot].astype(BF16)
        for j in range(tm):
            gather_row(nxt_ref, j, other).start(priority=0)
            scatter_row(j, other).start(priority=1)
        h = hbuf[...]
        gate = jnp.dot(h, wg_ref[...], preferred_element_type=F32)
        up = jnp.dot(h, wu_ref[...], preferred_element_type=F32)
        act = (gate * jax.nn.sigmoid(gate) * up).astype(BF16)
        ybuf[slot] = jnp.dot(act, wd_ref[...], preferred_element_type=F32)

    @pl.when(tv_ref[i] == 0)
    def _():
        gather_looped(nxt_ref, other)
        scatter_looped(other)

    @pl.when(i == last)
    def _():
        pltpu.make_async_copy(x_hbm.at[pl.ds(0, tm)], xbuf.at[other], gsem.at[other]).wait()
        pltpu.make_async_copy(ybuf.at[other], y_hbm.at[pl.ds(0, tm)], ssem.at[other]).wait()


def _grouped_swiglu(x, src, dst, tile_expert, tile_valid, tile_block, prev_block, wg, wu, wd, *, tm, y_rows):
    m, d = x.shape
    ff = wg.shape[2]
    n_tiles = tile_expert.shape[0]
    resident = pl.Buffered(1)
    smem_block = functools.partial(pl.BlockSpec, (tm,), memory_space=pltpu.SMEM)
    grid_spec = pltpu.PrefetchScalarGridSpec(
        num_scalar_prefetch=4,
        grid=(n_tiles,),
        in_specs=[smem_block(lambda i, te, tv, tb, tp: (tb[i],)),
                  smem_block(lambda i, te, tv, tb, tp: (tb[jnp.minimum(i + 1, n_tiles - 1)],)),
                  smem_block(lambda i, te, tv, tb, tp: (tp[i],)),
                  pl.BlockSpec(memory_space=pl.ANY),
                  pl.BlockSpec((None, d, ff), lambda i, te, tv, tb, tp: (te[i], 0, 0), pipeline_mode=resident),
                  pl.BlockSpec((None, d, ff), lambda i, te, tv, tb, tp: (te[i], 0, 0), pipeline_mode=resident),
                  pl.BlockSpec((None, ff, d), lambda i, te, tv, tb, tp: (te[i], 0, 0), pipeline_mode=resident)],
        out_specs=pl.BlockSpec(memory_space=pl.ANY),
        scratch_shapes=[pltpu.VMEM((2, tm, d), F32), pltpu.VMEM((tm, d), BF16), pltpu.VMEM((2, tm, d), F32),
                        pltpu.SemaphoreType.DMA((2,)), pltpu.SemaphoreType.DMA((2,))],
    )
    return pl.pallas_call(
        _grouped_kernel,
        grid_spec=grid_spec,
        out_shape=jax.ShapeDtypeStruct((y_rows, d), F32),
        compiler_params=_params(("arbitrary",), vmem=GROUPED_VMEM_LIMIT),
        name="grouped_swiglu",
    )(tile_expert, tile_valid, tile_block, prev_block, src, src, dst, x, wg, wu, wd)


def _combine_kernel(x_ref, meta_ref, g_ref, y0_ref, y1_ref, o_ref):
    meta = meta_ref[...]
    x = x_ref[...] + (meta[:, 4:5] * y0_ref[...] + meta[:, 5:6] * y1_ref[...])
    o_ref[...] = _rms(x, g_ref[...])


def _combine(x, meta, y, g, *, tm):
    m, d = x.shape
    return pl.pallas_call(
        _combine_kernel,
        grid=(m // tm,),
        in_specs=[pl.BlockSpec((tm, d), lambda i: (i, 0)),
                  pl.BlockSpec((tm, LANES), lambda i: (i, 0)),
                  pl.BlockSpec((1, d), lambda i: (0, 0)),
                  pl.BlockSpec((tm, d), lambda i: (i, 0)),
                  pl.BlockSpec((tm, d), lambda i: (m // tm + i, 0))],
        out_specs=pl.BlockSpec((tm, d), lambda i: (i, 0)),
        out_shape=jax.ShapeDtypeStruct((m, d), F32),
        compiler_params=_params(("arbitrary",)),
        name="combine",
    )(x, meta, g.reshape(1, d), y, y)


def _moe_layer(x, g, w_router, wg, wu, wd, final_g, *, tm):
    m, d = x.shape
    route_tm = 512
    meta, hn, starts, counts = _route(x, g, w_router, tm=route_tm)
    cnt = counts[0, :N_EXPERTS].astype(jnp.int32)
    tiles = (cnt + tm - 1) // tm
    tile_end = jnp.cumsum(tiles)
    first_tile = tile_end - tiles
    n_tiles = (2 * m) // tm + N_EXPERTS
    tile_id = jnp.arange(n_tiles, dtype=jnp.int32)
    tile_expert = jnp.sum(tile_id[:, None] >= tile_end[None, :], axis=1).astype(jnp.int32)
    tile_valid = (tile_expert < N_EXPERTS).astype(jnp.int32)
    tile_expert = jnp.minimum(tile_expert, N_EXPERTS - 1)
    cap = m + max(tm, route_tm)
    lists = _invert(meta, starts[:, 0, :N_EXPERTS].astype(jnp.int32).reshape(-1), tm=route_tm, cap=cap)
    code = (lists[:, 0] * 256.0 + lists[:, 1]).astype(jnp.int32) - 1
    src = jnp.maximum(code, 0) >> 1
    spare = 2 * m + jnp.arange(tm, dtype=jnp.int32)
    dst = jnp.where(code < 0, jnp.tile(spare, N_EXPERTS * cap // tm), (code & 1) * m + (code >> 1))
    trash_block = (N_EXPERTS * cap) // tm
    dst = jnp.concatenate([dst, spare])
    tile_block = tile_expert * (cap // tm) + tile_id - first_tile[tile_expert]
    dst_block = jnp.where(tile_valid > 0, tile_block, trash_block).astype(jnp.int32)
    prev_block = jnp.concatenate([jnp.full((1,), trash_block, jnp.int32), dst_block[:-1]])
    tile_block = jnp.where(tile_valid > 0, tile_block, 0).astype(jnp.int32)
    y = _grouped_swiglu(hn, src, dst, tile_expert, tile_valid, tile_block, prev_block, wg, wu, wd, tm=tm,
                        y_rows=2 * m + tm)
    return _combine(x, meta, y, final_g, tm=512)


def _mla_weights(w_q_up, w_kv_up):
    hd = QK_NOPE + QK_ROPE
    wq = w_q_up.reshape(Q_LORA, MLA_HEADS, hd)
    wq_pad = jnp.pad(wq, ((0, 0), (0, 0), (0, LANES - hd))).reshape(Q_LORA, MLA_HEADS * LANES)
    wkv = w_kv_up.reshape(KV_LORA, MLA_HEADS, QK_NOPE + V_DIM)
    wk_pad = jnp.pad(wkv[:, :, :QK_NOPE], ((0, 0), (0, 0), (0, LANES - QK_NOPE)))
    wk_pad = wk_pad.reshape(KV_LORA, MLA_HEADS * LANES)
    wv = jnp.pad(wkv[:, :, QK_NOPE:], ((0, 0), (0, 0), (0, LANES - V_DIM))).reshape(KV_LORA, MLA_HEADS * LANES)
    place = jnp.zeros((QK_ROPE, LANES), F32).at[jnp.arange(QK_ROPE), QK_NOPE + jnp.arange(QK_ROPE)].set(1.0)
    return wq_pad.astype(BF16), wk_pad.astype(BF16), place.astype(BF16), wv.astype(BF16)


def _rope_lane_tables(positions):
    half = QK_ROPE // 2
    inv = ROPE_THETA ** (-jnp.arange(0, QK_ROPE, 2, dtype=F32) / QK_ROPE)
    ang = positions.astype(F32)[..., None] * inv
    cos, sin = jnp.cos(ang), jnp.sin(ang)
    shape = positions.shape
    ones = jnp.ones(shape + (QK_NOPE,), F32)
    zeros_n = jnp.zeros(shape + (QK_NOPE,), F32)
    zeros_h = jnp.zeros(shape + (half,), F32)
    pad = jnp.zeros(shape + (LANES - QK_NOPE - QK_ROPE,), F32)
    c = jnp.concatenate([ones, cos, cos, pad], axis=-1)
    sa = jnp.concatenate([zeros_n, -sin, zeros_h, pad], axis=-1)
    sb = jnp.concatenate([zeros_n, zeros_h, sin, pad], axis=-1)
    return c, sa, sb


def kernel(x, positions, mix_norm_g, ffn_norm_g, a_w_qkv, a_w_out, b_w_in, b_q_norm_g, b_kv_norm_g,
           b_w_q_up, b_w_kv_up, b_w_out, ffn_w_gate, ffn_w_up, ffn_w_down, moe_w_router, moe_w_gate,
           moe_w_up, moe_w_down, final_norm_g):
    b, s, d = x.shape
    m = b * s
    xf = x.reshape(m, d)

    group_cols = 3 * DSA_WIDTH
    q_scale = jnp.where(jnp.arange(a_w_qkv.shape[2]) % group_cols < DSA_WIDTH, DSA_HEAD_DIM ** -0.5, 1.0)
    w_qkv = (a_w_qkv[0] * q_scale).astype(BF16)
    state = None
    for group, (_, dil) in enumerate(DIL_PAIRS):
        w_group = w_qkv[:, group * group_cols:(group + 1) * group_cols]
        qkv_v = _qkv_project(xf, mix_norm_g[0], w_group, dil=dil, tm=1024 if dil == 1 else 512)
        qkv_v = qkv_v.reshape(b, s // dil, dil * group_cols)
        state = _merge_group(qkv_v, positions, state, group=group, qb=4)
    x1 = _unpermute_project(xf, state, a_w_out[0].astype(BF16), dil=DIL_PAIRS[-1][1], tm=512)
    x2 = _swiglu(x1, ffn_norm_g[0], ffn_w_gate[0].astype(BF16), ffn_w_up[0].astype(BF16),
                 ffn_w_down[0].astype(BF16), tm=512)

    wq_pad, wk_pad, wkr, wv = _mla_weights(b_w_q_up[0], b_w_kv_up[0])
    q, k, v = _mla_project(x2.reshape(b, s, d), mix_norm_g[1], b_w_in[0].astype(BF16), b_q_norm_g[0],
                           b_kv_norm_g[0], wq_pad, wk_pad, wkr, wv, _rope_lane_tables(positions), tm=512)
    attn = _flash_attention(q, k, v, tk=512)
    x3 = _matmul_residual(x2, attn.reshape(m, MLA_HEADS * V_DIM), b_w_out[0].astype(BF16), tm=1024)
    out = _moe_layer(x3, ffn_norm_g[1], moe_w_router[0], moe_w_gate[0].astype(BF16),
                     moe_w_up[0].astype(BF16), moe_w_down[0].astype(BF16), final_norm_g,
                     tm=512)
    return out.reshape(b, s, d)
```

```python
import functools

import jax
import jax.numpy as jnp
from jax import lax
from jax.experimental import pallas as pl
from jax.experimental.pallas import tpu as pltpu

F32 = jnp.float32
BF16 = jnp.bfloat16

RMS_EPS = 1e-6
BLOCK = 128
LANES = 128
DIL_PAIRS = ((128, 1), (512, 4), (2048, 16))
DSA_HEADS = 8
DSA_HEAD_DIM = 64
DSA_WIDTH = DSA_HEADS * DSA_HEAD_DIM
MLA_HEADS = 16
QK_NOPE = 64
QK_ROPE = 32
V_DIM = 64
Q_LORA = 384
KV_LORA = 256
ROPE_THETA = 10000.0
N_EXPERTS = 8
MASK_VALUE = -1e30
LOG2_E = 1.4426950408889634
VMEM_LIMIT = 56 << 20
GROUPED_VMEM_LIMIT = 60 << 20


def _params(semantics, vmem=VMEM_LIMIT):
    return pltpu.CompilerParams(dimension_semantics=semantics, vmem_limit_bytes=vmem)


def _rms(x, g):
    ms = jnp.mean(x * x, axis=-1, keepdims=True)
    return x * lax.rsqrt(ms + RMS_EPS) * g


def _residue_slot(dil):
    if dil == 16:
        return tuple(4 * (r % 4) + r // 4 for r in range(dil))
    return tuple(range(dil))


def _deinterleave(tm, dil):
    n = tm // dil
    out_row = jnp.arange(tm, dtype=jnp.int32)
    source = (out_row % n) * dil + out_row // n
    return (source[:, None] == out_row[None, :]).astype(BF16)


def _qkv_kernel(x_ref, g_ref, perm_ref, w_ref, o_ref, *, dil):
    tm = x_ref.shape[0]
    n = tm // dil
    width = w_ref.shape[1]
    h = _rms(x_ref[...], g_ref[...]).astype(BF16)
    if dil > 1:
        h = jnp.dot(perm_ref[...], h, preferred_element_type=F32).astype(BF16)
    res = jnp.dot(h, w_ref[...], preferred_element_type=F32)
    for r, slot in enumerate(_residue_slot(dil)):
        o_ref[:, slot * width:(slot + 1) * width] = res[r * n:(r + 1) * n, :].astype(o_ref.dtype)


def _qkv_project(x, g, w, *, dil, tm):
    m, d = x.shape
    width = w.shape[1]
    n = tm // dil
    perm = _deinterleave(tm, dil)
    return pl.pallas_call(
        functools.partial(_qkv_kernel, dil=dil),
        grid=(m // tm,),
        in_specs=[pl.BlockSpec((tm, d), lambda i: (i, 0)),
                  pl.BlockSpec((1, d), lambda i: (0, 0)),
                  pl.BlockSpec((tm, tm), lambda i: (0, 0)),
                  pl.BlockSpec((d, width), lambda i: (0, 0))],
        out_specs=pl.BlockSpec((n, dil * width), lambda i: (i, 0)),
        out_shape=jax.ShapeDtypeStruct((m // dil, dil * width), BF16),
        compiler_params=_params(("arbitrary",)),
        name=f"qkv_project_d{dil}",
    )(x, g.reshape(1, d), perm, w)


def _matmul_residual_kernel(x_ref, a_ref, w_ref, o_ref):
    o_ref[...] = x_ref[...] + jnp.dot(a_ref[...], w_ref[...], preferred_element_type=F32)


def _matmul_residual(x, a, w, *, tm):
    m, d = x.shape
    k = a.shape[1]
    return pl.pallas_call(
        _matmul_residual_kernel,
        grid=(m // tm,),
        in_specs=[pl.BlockSpec((tm, d), lambda i: (i, 0)),
                  pl.BlockSpec((tm, k), lambda i: (i, 0)),
                  pl.BlockSpec((k, d), lambda i: (0, 0))],
        out_specs=pl.BlockSpec((tm, d), lambda i: (i, 0)),
        out_shape=jax.ShapeDtypeStruct((m, d), F32),
        compiler_params=_params(("arbitrary",)),
        name="matmul_residual",
    )(x, a, w)


def _merge_kernel(*refs, qb, span, has_prev, emit_lse):
    q_ref, k_ref, v_ref, prow_ref = refs[:4]
    pos = 4
    if has_prev:
        oprev_ref, lprev_ref = refs[pos:pos + 2]
        pos += 2
    if emit_lse:
        perm_ref, o_ref, lse_ref = refs[pos:pos + 3]
        pos += 3
    else:
        o_ref = refs[pos]
        pos += 1
    kbuf, vbuf, pbuf = refs[pos:pos + 3]
    if emit_lse:
        o_scr, lse_scr = refs[pos + 3:pos + 5]

    n = pl.program_id(2)
    tq = qb * BLOCK

    @pl.when(n == 0)
    def _():
        kbuf[0:BLOCK, :] = jnp.zeros((BLOCK, DSA_WIDTH), BF16)
        vbuf[0:BLOCK, :] = jnp.zeros((BLOCK, DSA_WIDTH), BF16)
        pbuf[:, 0:BLOCK] = jnp.zeros((1, BLOCK), jnp.int32)

    kbuf[BLOCK:, :] = k_ref[...]
    vbuf[BLOCK:, :] = v_ref[...]
    pbuf[:, BLOCK:] = prow_ref[...]

    lane = lax.broadcasted_iota(jnp.int32, (BLOCK, LANES), 1)
    sub = lax.broadcasted_iota(jnp.int32, (BLOCK, LANES), 0)
    low = lane < DSA_HEAD_DIM
    diag = lane == sub
    row = lax.broadcasted_iota(jnp.int32, (BLOCK, 2 * BLOCK), 0)
    col = lax.broadcasted_iota(jnp.int32, (BLOCK, 2 * BLOCK), 1)
    rel = row + BLOCK - col
    band = (rel >= 0) & (rel <= span)
    first_valid = band & ((col >= BLOCK) | (n > 0))
    spread = (lax.broadcasted_iota(jnp.int32, (LANES, DSA_WIDTH), 0)
              == lax.broadcasted_iota(jnp.int32, (LANES, DSA_WIDTH), 1) // DSA_HEAD_DIM).astype(BF16)

    def per_head(c):
        hi = c.astype(BF16)
        lo = (c - hi.astype(F32)).astype(BF16)
        return (jnp.dot(hi, spread, preferred_element_type=F32)
                + jnp.dot(lo, spread, preferred_element_type=F32))

    def lane_to_column(v):
        return jnp.sum(jnp.where(diag, v, 0.0), axis=1, keepdims=True)

    for i in range(qb):
        rows = slice(i * BLOCK, (i + 1) * BLOCK)
        keys = slice(i * BLOCK, (i + 2) * BLOCK)
        kk = kbuf[keys, :]
        vv = vbuf[keys, :]
        pk = pbuf[:, keys]
        pq_row = pbuf[:, (i + 1) * BLOCK:(i + 2) * BLOCK]
        pq = ((lane_to_column((pq_row >> 12).astype(F32)).astype(jnp.int32) << 12)
              | lane_to_column((pq_row & 4095).astype(F32)).astype(jnp.int32))
        dist = jnp.abs(pq - pk).astype(F32)
        neg_dist = jnp.where(first_valid if i == 0 else band, -dist, MASK_VALUE)
        q = q_ref[rows, :]
        m_tile = jnp.zeros((BLOCK, LANES), F32)
        l_tile = jnp.ones((BLOCK, LANES), F32)
        pv_pairs = []
        for p in range(DSA_HEADS // 2):
            cols = slice(p * LANES, (p + 1) * LANES)
            q2, k2, v2 = q[:, cols], kk[:, cols], vv[:, cols]
            pvs = []
            for half in range(2):
                h = 2 * p + half
                slope = 2.0 ** (-8.0 * (h + 1) / DSA_HEADS)
                qm = jnp.where(low if half == 0 else ~low, q2, jnp.zeros_like(q2))
                s = lax.dot_general(qm, k2, (((1,), (1,)), ((), ())), preferred_element_type=F32)
                s = s + slope * neg_dist
                m = jnp.max(s, axis=-1, keepdims=True)
                e = jnp.exp(s - m)
                m_tile = jnp.where(lane == h, m, m_tile)
                l_tile = jnp.where(lane == h, jnp.sum(e, axis=-1, keepdims=True), l_tile)
                pvs.append(jnp.dot(e.astype(BF16), v2, preferred_element_type=F32))
            pv_pairs.append(jnp.where(low, pvs[0], pvs[1]))
        pv_all = jnp.concatenate(pv_pairs, axis=1)
        lse = m_tile + jnp.log(l_tile)
        if has_prev:
            lp = lprev_ref[rows, :]
            mx = jnp.maximum(lp, lse)
            wa = jnp.exp(lp - mx)
            wb = jnp.exp(lse - mx)
            den = wa + wb
            c_new = wb / (den * l_tile)
            lse = mx + jnp.log(den)
            out = pv_all * per_head(c_new) + oprev_ref[rows, :].astype(F32) * per_head(wa / den)
        else:
            out = pv_all * per_head(1.0 / l_tile)
        if emit_lse:
            o_scr[rows, :] = out.astype(BF16)
            lse_scr[rows, :] = lse
        else:
            o_ref[rows, :] = out.astype(o_ref.dtype)

    if emit_lse:
        part = tq // 4
        o_perm = jnp.dot(perm_ref[...], o_scr[...], preferred_element_type=F32)
        for r in range(4):
            o_ref[:, r * DSA_WIDTH:(r + 1) * DSA_WIDTH] = o_perm[r * part:(r + 1) * part, :].astype(o_ref.dtype)
            lse_ref[:, r * LANES:(r + 1) * LANES] = lse_scr[pl.ds(r, part, stride=4), :]

    tail = slice(qb * BLOCK, (qb + 1) * BLOCK)
    kbuf[0:BLOCK, :] = kbuf[tail, :]
    vbuf[0:BLOCK, :] = vbuf[tail, :]
    pbuf[:, 0:BLOCK] = pbuf[:, tail]


def _merge_group(qkv_v, positions, prev, *, group, qb):
    b, s = positions.shape
    window, dil = DIL_PAIRS[group]
    last = group == len(DIL_PAIRS) - 1
    span = window // dil
    sub = s // dil
    tq = qb * BLOCK
    nb = sub // tq
    residue_of_slot = sorted(range(dil), key=_residue_slot(dil).__getitem__)
    prow = positions.reshape(b, sub, dil).transpose(0, 2, 1)[:, jnp.asarray(residue_of_slot), :]
    prow = prow.reshape(b, dil, 1, sub)

    def qkv_spec(which):
        return pl.BlockSpec((None, tq, DSA_WIDTH), lambda bi, r, n: (bi, n, r * 3 + which))

    in_specs = [qkv_spec(0), qkv_spec(1), qkv_spec(2),
                pl.BlockSpec((None, None, 1, tq), lambda bi, r, n: (bi, r, 0, n))]
    args = [qkv_v, qkv_v, qkv_v, prow]
    if prev is not None:
        in_specs += [pl.BlockSpec((None, tq, DSA_WIDTH), lambda bi, r, n: (bi, n, r)),
                     pl.BlockSpec((None, tq, LANES), lambda bi, r, n: (bi, n, r))]
        args += list(prev)
    scratch = [pltpu.VMEM((tq + BLOCK, DSA_WIDTH), BF16), pltpu.VMEM((tq + BLOCK, DSA_WIDTH), BF16),
               pltpu.VMEM((1, tq + BLOCK), jnp.int32)]
    if last:
        out_specs = [pl.BlockSpec((None, tq, DSA_WIDTH), lambda bi, r, n: (bi, n, r))]
        out_shape = [jax.ShapeDtypeStruct((b, sub, dil * DSA_WIDTH), BF16)]
    else:
        in_specs.append(pl.BlockSpec((tq, tq), lambda bi, r, n: (0, 0)))
        args.append(_deinterleave(tq, 4))
        out_specs = [pl.BlockSpec((None, tq // 4, 4 * DSA_WIDTH), lambda bi, r, n: (bi, n, r)),
                     pl.BlockSpec((None, tq // 4, 4 * LANES), lambda bi, r, n: (bi, n, r))]
        out_shape = [jax.ShapeDtypeStruct((b, sub // 4, 4 * dil * DSA_WIDTH), BF16),
                     jax.ShapeDtypeStruct((b, sub // 4, 4 * dil * LANES), F32)]
        scratch += [pltpu.VMEM((tq, DSA_WIDTH), BF16), pltpu.VMEM((tq, LANES), F32)]
    outs = pl.pallas_call(
        functools.partial(_merge_kernel, qb=qb, span=span, has_prev=prev is not None, emit_lse=not last),
        grid=(b, dil, nb),
        in_specs=in_specs,
        out_specs=out_specs,
        out_shape=out_shape,
        scratch_shapes=scratch,
        compiler_params=_params(("arbitrary", "arbitrary", "arbitrary")),
        name=f"dilated_group{group}",
    )(*args)
    return outs[0] if last else tuple(outs)


def _unpermute_project_kernel(x_ref, a_ref, perm_ref, w_ref, o_ref, *, dil):
    width = w_ref.shape[0]
    stacked = jnp.concatenate([a_ref[:, slot * width:(slot + 1) * width] for slot in _residue_slot(dil)],
                              axis=0)
    tokens = jnp.dot(perm_ref[...], stacked, preferred_element_type=F32).astype(BF16)
    o_ref[...] = x_ref[...] + jnp.dot(tokens, w_ref[...], preferred_element_type=F32)


def _unpermute_project(x, a, w, *, dil, tm):
    m, d = x.shape
    width = w.shape[0]
    return pl.pallas_call(
        functools.partial(_unpermute_project_kernel, dil=dil),
        grid=(m // tm,),
        in_specs=[pl.BlockSpec((tm, d), lambda i: (i, 0)),
                  pl.BlockSpec((tm // dil, dil * width), lambda i: (i, 0)),
                  pl.BlockSpec((tm, tm), lambda i: (0, 0)),
                  pl.BlockSpec((width, d), lambda i: (0, 0))],
        out_specs=pl.BlockSpec((tm, d), lambda i: (i, 0)),
        out_shape=jax.ShapeDtypeStruct((m, d), F32),
        compiler_params=_params(("arbitrary",)),
        name="unpermute_project",
    )(x, a.reshape(m // dil, dil * width), _deinterleave(tm, dil).T, w)


def _swiglu_kernel(x_ref, g_ref, wg_ref, wu_ref, wd_ref, o_ref):
    x = x_ref[...]
    h = _rms(x, g_ref[...]).astype(BF16)
    gate = jnp.dot(h, wg_ref[...], preferred_element_type=F32)
    up = jnp.dot(h, wu_ref[...], preferred_element_type=F32)
    act = (gate * jax.nn.sigmoid(gate) * up).astype(BF16)
    o_ref[...] = x + jnp.dot(act, wd_ref[...], preferred_element_type=F32)


def _swiglu(x, g, wg, wu, wd, *, tm):
    m, d = x.shape
    ff = wg.shape[1]
    resident = pl.Buffered(1)
    return pl.pallas_call(
        _swiglu_kernel,
        grid=(m // tm,),
        in_specs=[pl.BlockSpec((tm, d), lambda i: (i, 0)),
                  pl.BlockSpec((1, d), lambda i: (0, 0)),
                  pl.BlockSpec((d, ff), lambda i: (0, 0), pipeline_mode=resident),
                  pl.BlockSpec((d, ff), lambda i: (0, 0), pipeline_mode=resident),
                  pl.BlockSpec((ff, d), lambda i: (0, 0), pipeline_mode=resident)],
        out_specs=pl.BlockSpec((tm, d), lambda i: (i, 0)),
        out_shape=jax.ShapeDtypeStruct((m, d), F32),
        compiler_params=_params(("arbitrary",)),
        name="swiglu",
    )(x, g.reshape(1, d), wg, wu, wd)


def _rope_lanes(x, c, sa, sb):
    return x * c + pltpu.roll(x, LANES - QK_ROPE // 2, 1) * sa + pltpu.roll(x, QK_ROPE // 2, 1) * sb


def _mla_project_kernel(x_ref, g_ref, win_ref, gq_ref, gkv_ref, wq_ref, wk_ref, wkr_ref, wv_ref,
                        c_ref, sa_ref, sb_ref, q_ref, k_ref, v_ref, *, scale):
    h = _rms(x_ref[...], g_ref[...]).astype(BF16)
    a = jnp.dot(h, win_ref[...], preferred_element_type=F32)
    c_q = _rms(a[:, :Q_LORA], gq_ref[...]).astype(BF16)
    c_kv = _rms(a[:, Q_LORA:Q_LORA + KV_LORA], gkv_ref[...]).astype(BF16)
    k_rope = a[:, Q_LORA + KV_LORA:].astype(BF16)
    q = jnp.dot(c_q, wq_ref[...], preferred_element_type=F32)
    k = jnp.dot(c_kv, wk_ref[...], preferred_element_type=F32)
    v = jnp.dot(c_kv, wv_ref[...], preferred_element_type=F32)
    c, sa, sb = c_ref[...], sa_ref[...], sb_ref[...]
    ones_lane = lax.broadcasted_iota(jnp.int32, (1, 2 * LANES), 1) % LANES == V_DIM
    shared = _rope_lanes(jnp.dot(k_rope, wkr_ref[...], preferred_element_type=F32), c, sa, sb)
    for p in range(MLA_HEADS // 2):
        qs, ks = [], []
        for half in range(2):
            cols = slice((2 * p + half) * LANES, (2 * p + half + 1) * LANES)
            qs.append((_rope_lanes(q[:, cols], c, sa, sb) * scale).astype(BF16))
            ks.append((k[:, cols] + shared).astype(BF16))
        q_ref[p] = jnp.concatenate(qs, axis=1)
        k_ref[p] = jnp.concatenate(ks, axis=1)
        v_ref[p] = jnp.where(ones_lane, 1.0, v[:, 2 * p * LANES:2 * (p + 1) * LANES]).astype(BF16)


def _mla_project(x, g, w_in, gq, gkv, wq_pad, wk_pad, wkr, wv, tables, *, tm):
    b, s, d = x.shape
    c, sa, sb = tables
    pairs = MLA_HEADS // 2
    scale = (QK_NOPE + QK_ROPE) ** -0.5 * LOG2_E

    def full(arr):
        return pl.BlockSpec(arr.shape, lambda bi, i: (0,) * arr.ndim)

    tab = pl.BlockSpec((None, tm, LANES), lambda bi, i: (bi, i, 0))
    ops = [g.reshape(1, d), w_in, gq.reshape(1, -1), gkv.reshape(1, -1), wq_pad, wk_pad, wkr, wv]
    return pl.pallas_call(
        functools.partial(_mla_project_kernel, scale=scale),
        grid=(b, s // tm),
        in_specs=[pl.BlockSpec((None, tm, d), lambda bi, i: (bi, i, 0))] + [full(o) for o in ops]
                 + [tab, tab, tab],
        out_specs=[pl.BlockSpec((None, pairs, tm, 2 * LANES), lambda bi, i: (bi, 0, i, 0)),
                   pl.BlockSpec((None, pairs, tm, 2 * LANES), lambda bi, i: (bi, 0, i, 0)),
                   pl.BlockSpec((None, pairs, tm, 2 * LANES), lambda bi, i: (bi, 0, i, 0))],
        out_shape=[jax.ShapeDtypeStruct((b, pairs, s, 2 * LANES), BF16)] * 3,
        compiler_params=_params(("arbitrary", "arbitrary")),
        name="mla_project",
    )(x, *ops, c, sa, sb)


def _flash_kernel(q_ref, k_ref, v_ref, o_ref, s_scr, m_scr, acc_scr, *, tq, tk):
    qi = pl.program_id(2)
    m_scr[...] = jnp.full(m_scr.shape, MASK_VALUE, F32)
    acc_scr[...] = jnp.zeros(acc_scr.shape, F32)
    nt = (((1,), (1,)), ((), ()))

    def key_rows(kj):
        return pl.ds(pl.multiple_of(kj * tk, tk), tk)

    def head_cols(half):
        return slice(half * LANES, (half + 1) * LANES)

    def scores(kj, slot):
        ks = k_ref[key_rows(kj), :]
        for half in range(2):
            s_scr[slot, half] = lax.dot_general(q_ref[:, head_cols(half)], ks[:, head_cols(half)], nt,
                                                preferred_element_type=F32)

    def update(kj, rows, s_pair, triangular):
        n = rows.stop - rows.start
        vs = v_ref[key_rows(kj), :]
        for half in range(2):
            s = s_pair[half]
            if triangular:
                row = lax.broadcasted_iota(jnp.int32, (n, tk), 0)
                col = lax.broadcasted_iota(jnp.int32, (n, tk), 1)
                s = jnp.where(row >= col, s, MASK_VALUE)
            m_prev = m_scr[half, rows]
            m_new = jnp.maximum(m_prev, jnp.max(s, axis=-1, keepdims=True))
            e = jnp.exp2(s - jnp.tile(m_new, (1, tk // LANES)))
            m_scr[half, rows] = m_new
            pv = jnp.dot(e.astype(BF16), vs[:, head_cols(half)], preferred_element_type=F32)
            acc_scr[half, rows] = acc_scr[half, rows] * jnp.exp2(m_prev - m_new) + pv

    everything = slice(0, tq)
    top, bottom = slice(0, tk), slice(tk, tq)

    def from_scratch(slot, rows):
        return [s_scr[slot, half, rows] for half in range(2)]

    scores(0, 0)

    def body(jj, carry):
        scores(2 * jj + 1, 1)
        update(2 * jj, everything, from_scratch(0, everything), False)
        scores(2 * jj + 2, 0)
        update(2 * jj + 1, everything, from_scratch(1, everything), False)
        return carry

    lax.fori_loop(0, qi, body, 0)
    update(2 * qi, top, from_scratch(0, top), True)
    update(2 * qi, bottom, from_scratch(0, bottom), False)
    ks = k_ref[key_rows(2 * qi + 1), :]
    last = [lax.dot_general(q_ref[bottom, head_cols(half)], ks[:, head_cols(half)], nt,
                            preferred_element_type=F32) for half in range(2)]
    update(2 * qi + 1, bottom, last, True)
    out = [acc_scr[half] / acc_scr[half][:, V_DIM:V_DIM + 1] for half in range(2)]
    low = lax.broadcasted_iota(jnp.int32, (tq, LANES), 1) < V_DIM
    o_ref[...] = jnp.where(low, out[0], pltpu.roll(out[1], V_DIM, 1)).astype(o_ref.dtype)


def _flash_attention(q, k, v, *, tk):
    b, pairs, s, _ = q.shape
    tq = 2 * tk
    return pl.pallas_call(
        functools.partial(_flash_kernel, tq=tq, tk=tk),
        grid=(b, pairs, s // tq),
        in_specs=[pl.BlockSpec((None, None, tq, 2 * LANES), lambda bi, p, i: (bi, p, i, 0)),
                  pl.BlockSpec((None, None, s, 2 * LANES), lambda bi, p, i: (bi, p, 0, 0),
                               pipeline_mode=pl.Buffered(1)),
                  pl.BlockSpec((None, None, s, 2 * LANES), lambda bi, p, i: (bi, p, 0, 0),
                               pipeline_mode=pl.Buffered(1))],
        out_specs=pl.BlockSpec((None, tq, LANES), lambda bi, p, i: (bi, i, p)),
        out_shape=jax.ShapeDtypeStruct((b, s, pairs * LANES), BF16),
        scratch_shapes=[pltpu.VMEM((2, 2, tq, tk), F32),
                        pltpu.VMEM((2, tq, LANES), F32), pltpu.VMEM((2, tq, LANES), F32)],
        compiler_params=_params(("arbitrary", "arbitrary", "arbitrary")),
        name="flash_attention",
    )(q, k, v)


def _route_kernel(x_ref, g_ref, wr_ref, meta_ref, h_ref, start_ref, cnt_ref, carry):
    i = pl.program_id(0)
    t = x_ref.shape[0]

    @pl.when(i == 0)
    def _():
        carry[...] = jnp.zeros(carry.shape, F32)

    h = _rms(x_ref[...], g_ref[...])
    h_ref[...] = h
    h_hi = h.astype(BF16)
    h_lo = (h - h_hi.astype(F32)).astype(BF16)
    logits = (jnp.dot(h_hi, wr_ref[0], preferred_element_type=F32)
              + jnp.dot(h_hi, wr_ref[1], preferred_element_type=F32)
              + jnp.dot(h_lo, wr_ref[0], preferred_element_type=F32))
    lane = lax.broadcasted_iota(jnp.int32, (t, LANES), 1)
    lg = jnp.where(lane < N_EXPERTS, logits, -jnp.inf)
    m1 = jnp.max(lg, axis=-1, keepdims=True)
    i1 = jnp.min(jnp.where(lg == m1, lane, LANES), axis=-1, keepdims=True)
    lg2 = jnp.where(lane == i1, -jnp.inf, lg)
    m2 = jnp.max(lg2, axis=-1, keepdims=True)
    i2 = jnp.min(jnp.where(lg2 == m2, lane, LANES), axis=-1, keepdims=True)
    e2 = jnp.exp(m2 - m1)
    w1 = 1.0 / (1.0 + e2)
    w2 = e2 / (1.0 + e2)
    hot1 = lane == i1
    hot2 = lane == i2
    onehot = (hot1 | hot2).astype(BF16)
    r = lax.broadcasted_iota(jnp.int32, (t, t), 0)
    c = lax.broadcasted_iota(jnp.int32, (t, t), 1)
    lower = (c < r).astype(BF16)
    before = jnp.dot(lower, onehot, preferred_element_type=F32) + carry[...]
    start_ref[...] = carry[...]
    rank1 = jnp.sum(jnp.where(hot1, before, 0.0), axis=-1, keepdims=True)
    rank2 = jnp.sum(jnp.where(hot2, before, 0.0), axis=-1, keepdims=True)
    carry[...] += jnp.sum(onehot.astype(F32), axis=0, keepdims=True)
    meta = jnp.zeros((t, LANES), F32)
    for k, val in enumerate((i1.astype(F32), i2.astype(F32), rank1, rank2, w1, w2)):
        meta = jnp.where(lane == k, val, meta)
    meta_ref[...] = meta
    cnt_ref[...] = carry[...]


def _route(x, g, w_router, *, tm):
    m, d = x.shape
    wr = jnp.zeros((d, LANES), F32).at[:, :N_EXPERTS].set(w_router)
    wr_hi = wr.astype(BF16)
    wr = jnp.stack([wr_hi, (wr - wr_hi.astype(F32)).astype(BF16)])
    return pl.pallas_call(
        _route_kernel,
        grid=(m // tm,),
        in_specs=[pl.BlockSpec((tm, d), lambda i: (i, 0)),
                  pl.BlockSpec((1, d), lambda i: (0, 0)),
                  pl.BlockSpec((2, d, LANES), lambda i: (0, 0, 0))],
        out_specs=[pl.BlockSpec((tm, LANES), lambda i: (i, 0)),
                   pl.BlockSpec((tm, d), lambda i: (i, 0)),
                   pl.BlockSpec((None, 1, LANES), lambda i: (i, 0, 0)),
                   pl.BlockSpec((1, LANES), lambda i: (0, 0))],
        out_shape=[jax.ShapeDtypeStruct((m, LANES), F32),
                   jax.ShapeDtypeStruct((m, d), F32),
                   jax.ShapeDtypeStruct((m // tm, 1, LANES), F32),
                   jax.ShapeDtypeStruct((1, LANES), F32)],
        scratch_shapes=[pltpu.VMEM((1, LANES), F32)],
        compiler_params=_params(("arbitrary",)),
        name="route",
    )(x, g.reshape(1, d), wr)


def _invert_kernel(start_ref, base_ref, meta_ref, init_ref, list_ref, obuf, sem):
    del init_ref
    s = pl.program_id(0)
    t = meta_ref.shape[0]
    meta = meta_ref[...]
    second_expert = meta[:, 1:2]
    by_lane = meta.T
    i1, i2, r1, r2 = by_lane[0:1], by_lane[1:2], by_lane[2:3], by_lane[3:4]
    token = lax.broadcasted_iota(jnp.int32, (t, 1), 0) + s * t
    lane = lax.broadcasted_iota(jnp.int32, (t, LANES), 1)
    place = lax.broadcasted_iota(jnp.int32, (t, t), 0).astype(F32)
    copies = []
    for e in range(N_EXPERTS):
        first, second = i1 == e, i2 == e
        start = start_ref[s * N_EXPERTS + e]
        local = jnp.where(first, r1, r2) - start.astype(F32)
        onehot = ((local == place) & (first | second)).astype(BF16)
        code = 2 * token + (second_expert == e).astype(jnp.int32) + 1
        pieces = jnp.where(lane == 0, (code >> 8).astype(F32),
                           jnp.where(lane == 1, (code & 255).astype(F32), 0.0)).astype(BF16)
        obuf[e] = jnp.dot(onehot, pieces, preferred_element_type=F32)
        copy = pltpu.make_async_copy(obuf.at[e], list_ref.at[pl.ds(base_ref[e] + start, t)], sem)
        copy.start()
        copies.append(copy)
    for copy in copies:
        copy.wait()


def _invert(meta, starts, base, *, tm, rows):
    m = meta.shape[0]
    grid_spec = pltpu.PrefetchScalarGridSpec(
        num_scalar_prefetch=2,
        grid=(m // tm,),
        in_specs=[pl.BlockSpec((tm, LANES), lambda i, st, ba: (i, 0)),
                  pl.BlockSpec(memory_space=pl.ANY)],
        out_specs=pl.BlockSpec(memory_space=pl.ANY),
        scratch_shapes=[pltpu.VMEM((N_EXPERTS, tm, LANES), F32), pltpu.SemaphoreType.DMA(())],
    )
    return pl.pallas_call(
        _invert_kernel,
        grid_spec=grid_spec,
        out_shape=jax.ShapeDtypeStruct((rows, LANES), F32),
        input_output_aliases={3: 0},
        compiler_params=_params(("arbitrary",)),
        name="invert",
    )(starts, base, meta, jnp.zeros((rows, LANES), F32))


def _grouped_kernel(te_ref, tv_ref, tb_ref, tp_ref, cur_ref, nxt_ref, dprev_ref, x_hbm, wg_ref, wu_ref, wd_ref,
                    y_hbm, xbuf, hbuf, ybuf, gsem, ssem):
    del te_ref, tb_ref, tp_ref
    i = pl.program_id(0)
    last = pl.num_programs(0) - 1
    tm, d = xbuf.shape[1:]
    slot = i % 2
    other = 1 - slot

    def gather_row(ids_ref, j, buf):
        return pltpu.make_async_copy(x_hbm.at[pl.ds(ids_ref[j], 1)], xbuf.at[buf, pl.ds(j, 1)], gsem.at[buf])

    def scatter_row(j, buf):
        return pltpu.make_async_copy(ybuf.at[buf, pl.ds(j, 1)], y_hbm.at[pl.ds(dprev_ref[j], 1)], ssem.at[buf])

    def gather_looped(ids_ref, buf):
        def body(j, carry):
            gather_row(ids_ref, j, buf).start()
            return carry
        lax.fori_loop(0, tm, body, 0, unroll=8)

    def scatter_looped(buf):
        def body(j, carry):
            scatter_row(j, buf).start()
            return carry
        lax.fori_loop(0, tm, body, 0, unroll=8)

    @pl.when(i == 0)
    def _():
        ybuf[...] = jnp.zeros(ybuf.shape, F32)
        gather_looped(cur_ref, 0)

    pltpu.make_async_copy(x_hbm.at[pl.ds(0, tm)], xbuf.at[slot], gsem.at[slot]).wait()

    @pl.when(i >= 1)
    def _():
        pltpu.make_async_copy(ybuf.at[slot], y_hbm.at[pl.ds(0, tm)], ssem.at[slot]).wait()

    @pl.when(tv_ref[i] > 0)
    def _():
        hbuf[...] = xbuf[slot].astype(BF16)
        for j in range(tm // 2, tm):
            gather_row(nxt_ref, j, other).start(priority=0)
            scatter_row(j, other).start(priority=1)
        h = hbuf[...]
        gate = jnp.dot(h, wg_ref[...], preferred_element_type=F32)
        up = jnp.dot(h, wu_ref[...], preferred_element_type=F32)
        act = (gate * jax.nn.sigmoid(gate) * up).astype(BF16)
        never = lax.broadcasted_iota(jnp.int32, (8, d), 1) < (i >> 30)
        late = gate[0:8, gate.shape[1] - d:]
        ybuf[other, 0:8, :] = jnp.where(never, late, ybuf[other, 0:8, :])
        xbuf[other, 0:8, :] = jnp.where(never, late, 0.0)
        for j in range(tm // 2):
            gather_row(nxt_ref, j, other).start(priority=0)
            scatter_row(j, other).start(priority=1)
        ybuf[slot] = jnp.dot(act, wd_ref[...], preferred_element_type=F32)

    @pl.when(tv_ref[i] == 0)
    def _():
        gather_looped(nxt_ref, other)
        scatter_looped(other)

    @pl.when(i == last)
    def _():
        pltpu.make_async_copy(x_hbm.at[pl.ds(0, tm)], xbuf.at[other], gsem.at[other]).wait()
        pltpu.make_async_copy(ybuf.at[other], y_hbm.at[pl.ds(0, tm)], ssem.at[other]).wait()


def _grouped_swiglu(x, src, dst, tile_expert, tile_valid, tile_block, prev_block, wg, wu, wd, *, tm, y_rows):
    m, d = x.shape
    ff = wg.shape[2]
    n_tiles = tile_expert.shape[0]
    resident = pl.Buffered(1)
    smem_block = functools.partial(pl.BlockSpec, (tm,), memory_space=pltpu.SMEM)
    grid_spec = pltpu.PrefetchScalarGridSpec(
        num_scalar_prefetch=4,
        grid=(n_tiles,),
        in_specs=[smem_block(lambda i, te, tv, tb, tp: (tb[i],)),
                  smem_block(lambda i, te, tv, tb, tp: (tb[jnp.minimum(i + 1, n_tiles - 1)],)),
                  smem_block(lambda i, te, tv, tb, tp: (tp[i],)),
                  pl.BlockSpec(memory_space=pl.ANY),
                  pl.BlockSpec((None, d, ff), lambda i, te, tv, tb, tp: (te[i], 0, 0), pipeline_mode=resident),
                  pl.BlockSpec((None, d, ff), lambda i, te, tv, tb, tp: (te[i], 0, 0), pipeline_mode=resident),
                  pl.BlockSpec((None, ff, d), lambda i, te, tv, tb, tp: (te[i], 0, 0), pipeline_mode=resident)],
        out_specs=pl.BlockSpec(memory_space=pl.ANY),
        scratch_shapes=[pltpu.VMEM((2, tm, d), F32), pltpu.VMEM((tm, d), BF16), pltpu.VMEM((2, tm, d), F32),
                        pltpu.SemaphoreType.DMA((2,)), pltpu.SemaphoreType.DMA((2,))],
    )
    return pl.pallas_call(
        _grouped_kernel,
        grid_spec=grid_spec,
        out_shape=jax.ShapeDtypeStruct((y_rows, d), F32),
        compiler_params=_params(("arbitrary",), vmem=GROUPED_VMEM_LIMIT),
        name="grouped_swiglu",
    )(tile_expert, tile_valid, tile_block, prev_block, src, src, dst, x, wg, wu, wd)


def _combine_kernel(x_ref, meta_ref, g_ref, y0_ref, y1_ref, o_ref):
    meta = meta_ref[...]
    x = x_ref[...] + (meta[:, 4:5] * y0_ref[...] + meta[:, 5:6] * y1_ref[...])
    o_ref[...] = _rms(x, g_ref[...])


def _combine(x, meta, y, g, *, tm):
    m, d = x.shape
    return pl.pallas_call(
        _combine_kernel,
        grid=(m // tm,),
        in_specs=[pl.BlockSpec((tm, d), lambda i: (i, 0)),
                  pl.BlockSpec((tm, LANES), lambda i: (i, 0)),
                  pl.BlockSpec((1, d), lambda i: (0, 0)),
                  pl.BlockSpec((tm, d), lambda i: (i, 0)),
                  pl.BlockSpec((tm, d), lambda i: (m // tm + i, 0))],
        out_specs=pl.BlockSpec((tm, d), lambda i: (i, 0)),
        out_shape=jax.ShapeDtypeStruct((m, d), F32),
        compiler_params=_params(("arbitrary",)),
        name="combine",
    )(x, meta, g.reshape(1, d), y, y)


def _moe_layer(x, g, w_router, wg, wu, wd, final_g, *, tm):
    m, d = x.shape
    route_tm = 512
    meta, hn, starts, counts = _route(x, g, w_router, tm=route_tm)
    cnt = counts[0, :N_EXPERTS].astype(jnp.int32)
    tiles = (cnt + tm - 1) // tm
    tile_end = jnp.cumsum(tiles)
    first_tile = tile_end - tiles
    n_tiles = (2 * m) // tm + N_EXPERTS
    tile_id = jnp.arange(n_tiles, dtype=jnp.int32)
    tile_expert = jnp.sum(tile_id[:, None] >= tile_end[None, :], axis=1).astype(jnp.int32)
    tile_valid = (tile_expert < N_EXPERTS).astype(jnp.int32)
    tile_expert = jnp.minimum(tile_expert, N_EXPERTS - 1)
    assert route_tm <= tm
    base = (first_tile + jnp.arange(N_EXPERTS, dtype=jnp.int32)) * tm
    list_blocks = n_tiles + N_EXPERTS
    lists = _invert(meta, starts[:, 0, :N_EXPERTS].astype(jnp.int32).reshape(-1), base.astype(jnp.int32),
                    tm=route_tm, rows=list_blocks * tm)
    code = (lists[:, 0] * 256.0 + lists[:, 1]).astype(jnp.int32) - 1
    src = jnp.maximum(code, 0) >> 1
    spare = 2 * m + jnp.arange(tm, dtype=jnp.int32)
    dst = jnp.where(code < 0, jnp.tile(spare, list_blocks), (code & 1) * m + (code >> 1))
    empty_block = list_blocks - 1
    tile_block = tile_id + tile_expert
    dst_block = jnp.where(tile_valid > 0, tile_block, empty_block).astype(jnp.int32)
    prev_block = jnp.concatenate([jnp.full((1,), empty_block, jnp.int32), dst_block[:-1]])
    tile_block = jnp.where(tile_valid > 0, tile_block, empty_block).astype(jnp.int32)
    y = _grouped_swiglu(hn, src, dst, tile_expert, tile_valid, tile_block, prev_block, wg, wu, wd, tm=tm,
                        y_rows=2 * m + tm)
    return _combine(x, meta, y, final_g, tm=512)


def _mla_weights(w_q_up, w_kv_up):
    hd = QK_NOPE + QK_ROPE
    wq = w_q_up.reshape(Q_LORA, MLA_HEADS, hd)
    wq_pad = jnp.pad(wq, ((0, 0), (0, 0), (0, LANES - hd))).reshape(Q_LORA, MLA_HEADS * LANES)
    wkv = w_kv_up.reshape(KV_LORA, MLA_HEADS, QK_NOPE + V_DIM)
    wk_pad = jnp.pad(wkv[:, :, :QK_NOPE], ((0, 0), (0, 0), (0, LANES - QK_NOPE)))
    wk_pad = wk_pad.reshape(KV_LORA, MLA_HEADS * LANES)
    wv = jnp.pad(wkv[:, :, QK_NOPE:], ((0, 0), (0, 0), (0, LANES - V_DIM))).reshape(KV_LORA, MLA_HEADS * LANES)
    place = jnp.zeros((QK_ROPE, LANES), F32).at[jnp.arange(QK_ROPE), QK_NOPE + jnp.arange(QK_ROPE)].set(1.0)
    return wq_pad.astype(BF16), wk_pad.astype(BF16), place.astype(BF16), wv.astype(BF16)


def _rope_lane_tables(positions):
    half = QK_ROPE // 2
    inv = ROPE_THETA ** (-jnp.arange(0, QK_ROPE, 2, dtype=F32) / QK_ROPE)
    ang = positions.astype(F32)[..., None] * inv
    cos, sin = jnp.cos(ang), jnp.sin(ang)
    shape = positions.shape
    ones = jnp.ones(shape + (QK_NOPE,), F32)
    zeros_n = jnp.zeros(shape + (QK_NOPE,), F32)
    zeros_h = jnp.zeros(shape + (half,), F32)
    pad = jnp.zeros(shape + (LANES - QK_NOPE - QK_ROPE,), F32)
    c = jnp.concatenate([ones, cos, cos, pad], axis=-1)
    sa = jnp.concatenate([zeros_n, -sin, zeros_h, pad], axis=-1)
    sb = jnp.concatenate([zeros_n, zeros_h, sin, pad], axis=-1)
    return c, sa, sb


def kernel(x, positions, mix_norm_g, ffn_norm_g, a_w_qkv, a_w_out, b_w_in, b_q_norm_g, b_kv_norm_g,
           b_w_q_up, b_w_kv_up, b_w_out, ffn_w_gate, ffn_w_up, ffn_w_down, moe_w_router, moe_w_gate,
           moe_w_up, moe_w_down, final_norm_g):
    b, s, d = x.shape
    m = b * s
    xf = x.reshape(m, d)

    group_cols = 3 * DSA_WIDTH
    q_scale = jnp.where(jnp.arange(a_w_qkv.shape[2]) % group_cols < DSA_WIDTH, DSA_HEAD_DIM ** -0.5, 1.0)
    w_qkv = (a_w_qkv[0] * q_scale).astype(BF16)
    state = None
    for group, (_, dil) in enumerate(DIL_PAIRS):
        w_group = w_qkv[:, group * group_cols:(group + 1) * group_cols]
        qkv_v = _qkv_project(xf, mix_norm_g[0], w_group, dil=dil, tm=1024 if dil == 1 else 512)
        qkv_v = qkv_v.reshape(b, s // dil, dil * group_cols)
        state = _merge_group(qkv_v, positions, state, group=group, qb=4)
    x1 = _unpermute_project(xf, state, a_w_out[0].astype(BF16), dil=DIL_PAIRS[-1][1], tm=512)
    x2 = _swiglu(x1, ffn_norm_g[0], ffn_w_gate[0].astype(BF16), ffn_w_up[0].astype(BF16),
                 ffn_w_down[0].astype(BF16), tm=512)

    wq_pad, wk_pad, wkr, wv = _mla_weights(b_w_q_up[0], b_w_kv_up[0])
    q, k, v = _mla_project(x2.reshape(b, s, d), mix_norm_g[1], b_w_in[0].astype(BF16), b_q_norm_g[0],
                           b_kv_norm_g[0], wq_pad, wk_pad, wkr, wv, _rope_lane_tables(positions), tm=512)
    attn = _flash_attention(q, k, v, tk=512)
    x3 = _matmul_residual(x2, attn.reshape(m, MLA_HEADS * V_DIM), b_w_out[0].astype(BF16), tm=1024)
    out = _moe_layer(x3, ffn_norm_g[1], moe_w_router[0], moe_w_gate[0].astype(BF16),
                     moe_w_up[0].astype(BF16), moe_w_down[0].astype(BF16), final_norm_g,
                     tm=512)
    return out.reshape(b, s, d)
```

```python
import functools

import jax
import jax.numpy as jnp
from jax import lax
from jax.experimental import pallas as pl
from jax.experimental.pallas import tpu as pltpu

F32 = jnp.float32
BF16 = jnp.bfloat16

RMS_EPS = 1e-6
BLOCK = 128
LANES = 128
DIL_PAIRS = ((128, 1), (512, 4), (2048, 16))
DSA_HEADS = 8
DSA_HEAD_DIM = 64
DSA_WIDTH = DSA_HEADS * DSA_HEAD_DIM
MLA_HEADS = 16
QK_NOPE = 64
QK_ROPE = 32
V_DIM = 64
Q_LORA = 384
KV_LORA = 256
ROPE_THETA = 10000.0
N_EXPERTS = 8
MASK_VALUE = -1e30
LOG2_E = 1.4426950408889634
VMEM_LIMIT = 56 << 20
GROUPED_VMEM_LIMIT = 60 << 20


def _params(semantics, vmem=VMEM_LIMIT):
    return pltpu.CompilerParams(dimension_semantics=semantics, vmem_limit_bytes=vmem)


def _rms(x, g):
    ms = jnp.mean(x * x, axis=-1, keepdims=True)
    return x * lax.rsqrt(ms + RMS_EPS) * g


def _residue_slot(dil):
    if dil == 16:
        return tuple(4 * (r % 4) + r // 4 for r in range(dil))
    return tuple(range(dil))


def _deinterleave(tm, dil):
    n = tm // dil
    out_row = jnp.arange(tm, dtype=jnp.int32)
    source = (out_row % n) * dil + out_row // n
    return (source[:, None] == out_row[None, :]).astype(BF16)


def _qkv_kernel(x_ref, g_ref, perm_ref, w_ref, o_ref, *, dil):
    tm = x_ref.shape[0]
    n = tm // dil
    width = w_ref.shape[1]
    h = _rms(x_ref[...], g_ref[...]).astype(BF16)
    if dil > 1:
        h = jnp.dot(perm_ref[...], h, preferred_element_type=F32).astype(BF16)
    res = jnp.dot(h, w_ref[...], preferred_element_type=F32)
    for r, slot in enumerate(_residue_slot(dil)):
        o_ref[:, slot * width:(slot + 1) * width] = res[r * n:(r + 1) * n, :].astype(o_ref.dtype)


def _qkv_project(x, g, w, *, dil, tm):
    m, d = x.shape
    width = w.shape[1]
    n = tm // dil
    perm = _deinterleave(tm, dil)
    return pl.pallas_call(
        functools.partial(_qkv_kernel, dil=dil),
        grid=(m // tm,),
        in_specs=[pl.BlockSpec((tm, d), lambda i: (i, 0)),
                  pl.BlockSpec((1, d), lambda i: (0, 0)),
                  pl.BlockSpec((tm, tm), lambda i: (0, 0)),
                  pl.BlockSpec((d, width), lambda i: (0, 0))],
        out_specs=pl.BlockSpec((n, dil * width), lambda i: (i, 0)),
        out_shape=jax.ShapeDtypeStruct((m // dil, dil * width), BF16),
        compiler_params=_params(("arbitrary",)),
        name=f"qkv_project_d{dil}",
    )(x, g.reshape(1, d), perm, w)


def _matmul_residual_kernel(x_ref, a_ref, w_ref, o_ref):
    o_ref[...] = x_ref[...] + jnp.dot(a_ref[...], w_ref[...], preferred_element_type=F32)


def _matmul_residual(x, a, w, *, tm):
    m, d = x.shape
    k = a.shape[1]
    return pl.pallas_call(
        _matmul_residual_kernel,
        grid=(m // tm,),
        in_specs=[pl.BlockSpec((tm, d), lambda i: (i, 0)),
                  pl.BlockSpec((tm, k), lambda i: (i, 0)),
                  pl.BlockSpec((k, d), lambda i: (0, 0))],
        out_specs=pl.BlockSpec((tm, d), lambda i: (i, 0)),
        out_shape=jax.ShapeDtypeStruct((m, d), F32),
        compiler_params=_params(("arbitrary",)),
        name="matmul_residual",
    )(x, a, w)


def _merge_kernel(*refs, qb, span, has_prev, emit_lse):
    q_ref, k_ref, v_ref, prow_ref = refs[:4]
    pos = 4
    if has_prev:
        oprev_ref, lprev_ref = refs[pos:pos + 2]
        pos += 2
    if emit_lse:
        perm_ref, o_ref, lse_ref = refs[pos:pos + 3]
        pos += 3
    else:
        o_ref = refs[pos]
        pos += 1
    kbuf, vbuf, pbuf = refs[pos:pos + 3]
    if emit_lse:
        o_scr, lse_scr = refs[pos + 3:pos + 5]

    n = pl.program_id(2)
    tq = qb * BLOCK

    @pl.when(n == 0)
    def _():
        kbuf[0:BLOCK, :] = jnp.zeros((BLOCK, DSA_WIDTH), BF16)
        vbuf[0:BLOCK, :] = jnp.zeros((BLOCK, DSA_WIDTH), BF16)
        pbuf[:, 0:BLOCK] = jnp.zeros((1, BLOCK), jnp.int32)

    kbuf[BLOCK:, :] = k_ref[...]
    vbuf[BLOCK:, :] = v_ref[...]
    pbuf[:, BLOCK:] = prow_ref[...]

    lane = lax.broadcasted_iota(jnp.int32, (BLOCK, LANES), 1)
    sub = lax.broadcasted_iota(jnp.int32, (BLOCK, LANES), 0)
    low = lane < DSA_HEAD_DIM
    diag = lane == sub
    row = lax.broadcasted_iota(jnp.int32, (BLOCK, 2 * BLOCK), 0)
    col = lax.broadcasted_iota(jnp.int32, (BLOCK, 2 * BLOCK), 1)
    rel = row + BLOCK - col
    band = (rel >= 0) & (rel <= span)
    first_valid = band & ((col >= BLOCK) | (n > 0))
    spread = (lax.broadcasted_iota(jnp.int32, (LANES, DSA_WIDTH), 0)
              == lax.broadcasted_iota(jnp.int32, (LANES, DSA_WIDTH), 1) // DSA_HEAD_DIM).astype(BF16)

    def per_head(c):
        hi = c.astype(BF16)
        lo = (c - hi.astype(F32)).astype(BF16)
        return (jnp.dot(hi, spread, preferred_element_type=F32)
                + jnp.dot(lo, spread, preferred_element_type=F32))

    def lane_to_column(v):
        return jnp.sum(jnp.where(diag, v, 0.0), axis=1, keepdims=True)

    for i in range(qb):
        rows = slice(i * BLOCK, (i + 1) * BLOCK)
        keys = slice(i * BLOCK, (i + 2) * BLOCK)
        kk = kbuf[keys, :]
        vv = vbuf[keys, :]
        pk = pbuf[:, keys]
        pq_row = pbuf[:, (i + 1) * BLOCK:(i + 2) * BLOCK]
        pq = ((lane_to_column((pq_row >> 12).astype(F32)).astype(jnp.int32) << 12)
              | lane_to_column((pq_row & 4095).astype(F32)).astype(jnp.int32))
        dist = jnp.abs(pq - pk).astype(F32)
        neg_dist = jnp.where(first_valid if i == 0 else band, -dist, MASK_VALUE)
        q = q_ref[rows, :]
        m_tile = jnp.zeros((BLOCK, LANES), F32)
        l_tile = jnp.ones((BLOCK, LANES), F32)
        pv_pairs = []
        for p in range(DSA_HEADS // 2):
            cols = slice(p * LANES, (p + 1) * LANES)
            q2, k2, v2 = q[:, cols], kk[:, cols], vv[:, cols]
            pvs = []
            for half in range(2):
                h = 2 * p + half
                slope = 2.0 ** (-8.0 * (h + 1) / DSA_HEADS)
                qm = jnp.where(low if half == 0 else ~low, q2, jnp.zeros_like(q2))
                s = lax.dot_general(qm, k2, (((1,), (1,)), ((), ())), preferred_element_type=F32)
                s = s + slope * neg_dist
                m = jnp.max(s, axis=-1, keepdims=True)
                e = jnp.exp(s - m)
                m_tile = jnp.where(lane == h, m, m_tile)
                l_tile = jnp.where(lane == h, jnp.sum(e, axis=-1, keepdims=True), l_tile)
                pvs.append(jnp.dot(e.astype(BF16), v2, preferred_element_type=F32))
            pv_pairs.append(jnp.where(low, pvs[0], pvs[1]))
        pv_all = jnp.concatenate(pv_pairs, axis=1)
        lse = m_tile + jnp.log(l_tile)
        if has_prev:
            lp = lprev_ref[rows, :]
            mx = jnp.maximum(lp, lse)
            wa = jnp.exp(lp - mx)
            wb = jnp.exp(lse - mx)
            den = wa + wb
            c_new = wb / (den * l_tile)
            lse = mx + jnp.log(den)
            out = pv_all * per_head(c_new) + oprev_ref[rows, :].astype(F32) * per_head(wa / den)
        else:
            out = pv_all * per_head(1.0 / l_tile)
        if emit_lse:
            o_scr[rows, :] = out.astype(BF16)
            lse_scr[rows, :] = lse
        else:
            o_ref[rows, :] = out.astype(o_ref.dtype)

    if emit_lse:
        part = tq // 4
        o_perm = jnp.dot(perm_ref[...], o_scr[...], preferred_element_type=F32)
        for r in range(4):
            o_ref[:, r * DSA_WIDTH:(r + 1) * DSA_WIDTH] = o_perm[r * part:(r + 1) * part, :].astype(o_ref.dtype)
            lse_ref[:, r * LANES:(r + 1) * LANES] = lse_scr[pl.ds(r, part, stride=4), :]

    tail = slice(qb * BLOCK, (qb + 1) * BLOCK)
    kbuf[0:BLOCK, :] = kbuf[tail, :]
    vbuf[0:BLOCK, :] = vbuf[tail, :]
    pbuf[:, 0:BLOCK] = pbuf[:, tail]


def _merge_group(qkv_v, positions, prev, *, group, qb):
    b, s = positions.shape
    window, dil = DIL_PAIRS[group]
    last = group == len(DIL_PAIRS) - 1
    span = window // dil
    sub = s // dil
    tq = qb * BLOCK
    nb = sub // tq
    residue_of_slot = sorted(range(dil), key=_residue_slot(dil).__getitem__)
    prow = positions.reshape(b, sub, dil).transpose(0, 2, 1)[:, jnp.asarray(residue_of_slot), :]
    prow = prow.reshape(b, dil, 1, sub)

    def qkv_spec(which):
        return pl.BlockSpec((None, tq, DSA_WIDTH), lambda bi, r, n: (bi, n, r * 3 + which))

    in_specs = [qkv_spec(0), qkv_spec(1), qkv_spec(2),
                pl.BlockSpec((None, None, 1, tq), lambda bi, r, n: (bi, r, 0, n))]
    args = [qkv_v, qkv_v, qkv_v, prow]
    if prev is not None:
        in_specs += [pl.BlockSpec((None, tq, DSA_WIDTH), lambda bi, r, n: (bi, n, r)),
                     pl.BlockSpec((None, tq, LANES), lambda bi, r, n: (bi, n, r))]
        args += list(prev)
    scratch = [pltpu.VMEM((tq + BLOCK, DSA_WIDTH), BF16), pltpu.VMEM((tq + BLOCK, DSA_WIDTH), BF16),
               pltpu.VMEM((1, tq + BLOCK), jnp.int32)]
    if last:
        out_specs = [pl.BlockSpec((None, tq, DSA_WIDTH), lambda bi, r, n: (bi, n, r))]
        out_shape = [jax.ShapeDtypeStruct((b, sub, dil * DSA_WIDTH), BF16)]
    else:
        in_specs.append(pl.BlockSpec((tq, tq), lambda bi, r, n: (0, 0)))
        args.append(_deinterleave(tq, 4))
        out_specs = [pl.BlockSpec((None, tq // 4, 4 * DSA_WIDTH), lambda bi, r, n: (bi, n, r)),
                     pl.BlockSpec((None, tq // 4, 4 * LANES), lambda bi, r, n: (bi, n, r))]
        out_shape = [jax.ShapeDtypeStruct((b, sub // 4, 4 * dil * DSA_WIDTH), BF16),
                     jax.ShapeDtypeStruct((b, sub // 4, 4 * dil * LANES), F32)]
        scratch += [pltpu.VMEM((tq, DSA_WIDTH), BF16), pltpu.VMEM((tq, LANES), F32)]
    outs = pl.pallas_call(
        functools.partial(_merge_kernel, qb=qb, span=span, has_prev=prev is not None, emit_lse=not last),
        grid=(b, dil, nb),
        in_specs=in_specs,
        out_specs=out_specs,
        out_shape=out_shape,
        scratch_shapes=scratch,
        compiler_params=_params(("arbitrary", "arbitrary", "arbitrary")),
        name=f"dilated_group{group}",
    )(*args)
    return outs[0] if last else tuple(outs)


def _unpermute_project_kernel(x_ref, a_ref, perm_ref, w_ref, o_ref, *, dil):
    width = w_ref.shape[0]
    stacked = jnp.concatenate([a_ref[:, slot * width:(slot + 1) * width] for slot in _residue_slot(dil)],
                              axis=0)
    tokens = jnp.dot(perm_ref[...], stacked, preferred_element_type=F32).astype(BF16)
    o_ref[...] = x_ref[...] + jnp.dot(tokens, w_ref[...], preferred_element_type=F32)


def _unpermute_project(x, a, w, *, dil, tm):
    m, d = x.shape
    width = w.shape[0]
    return pl.pallas_call(
        functools.partial(_unpermute_project_kernel, dil=dil),
        grid=(m // tm,),
        in_specs=[pl.BlockSpec((tm, d), lambda i: (i, 0)),
                  pl.BlockSpec((tm // dil, dil * width), lambda i: (i, 0)),
                  pl.BlockSpec((tm, tm), lambda i: (0, 0)),
                  pl.BlockSpec((width, d), lambda i: (0, 0))],
        out_specs=pl.BlockSpec((tm, d), lambda i: (i, 0)),
        out_shape=jax.ShapeDtypeStruct((m, d), F32),
        compiler_params=_params(("arbitrary",)),
        name="unpermute_project",
    )(x, a.reshape(m // dil, dil * width), _deinterleave(tm, dil).T, w)


def _swiglu_kernel(x_ref, g_ref, wg_ref, wu_ref, wd_ref, o_ref):
    x = x_ref[...]
    h = _rms(x, g_ref[...]).astype(BF16)
    gate = jnp.dot(h, wg_ref[...], preferred_element_type=F32)
    up = jnp.dot(h, wu_ref[...], preferred_element_type=F32)
    act = (gate * jax.nn.sigmoid(gate) * up).astype(BF16)
    o_ref[...] = x + jnp.dot(act, wd_ref[...], preferred_element_type=F32)


def _swiglu(x, g, wg, wu, wd, *, tm):
    m, d = x.shape
    ff = wg.shape[1]
    resident = pl.Buffered(1)
    return pl.pallas_call(
        _swiglu_kernel,
        grid=(m // tm,),
        in_specs=[pl.BlockSpec((tm, d), lambda i: (i, 0)),
                  pl.BlockSpec((1, d), lambda i: (0, 0)),
                  pl.BlockSpec((d, ff), lambda i: (0, 0), pipeline_mode=resident),
                  pl.BlockSpec((d, ff), lambda i: (0, 0), pipeline_mode=resident),
                  pl.BlockSpec((ff, d), lambda i: (0, 0), pipeline_mode=resident)],
        out_specs=pl.BlockSpec((tm, d), lambda i: (i, 0)),
        out_shape=jax.ShapeDtypeStruct((m, d), F32),
        compiler_params=_params(("arbitrary",)),
        name="swiglu",
    )(x, g.reshape(1, d), wg, wu, wd)


def _rope_lanes(x, c, sa, sb):
    return x * c + pltpu.roll(x, LANES - QK_ROPE // 2, 1) * sa + pltpu.roll(x, QK_ROPE // 2, 1) * sb


def _mla_project_kernel(x_ref, g_ref, win_ref, gq_ref, gkv_ref, wq_ref, wk_ref, wkr_ref, wv_ref,
                        c_ref, sa_ref, sb_ref, q_ref, k_ref, v_ref, *, scale):
    h = _rms(x_ref[...], g_ref[...]).astype(BF16)
    a = jnp.dot(h, win_ref[...], preferred_element_type=F32)
    c_q = _rms(a[:, :Q_LORA], gq_ref[...]).astype(BF16)
    c_kv = _rms(a[:, Q_LORA:Q_LORA + KV_LORA], gkv_ref[...]).astype(BF16)
    k_rope = a[:, Q_LORA + KV_LORA:].astype(BF16)
    q = jnp.dot(c_q, wq_ref[...], preferred_element_type=F32)
    k = jnp.dot(c_kv, wk_ref[...], preferred_element_type=F32)
    v = jnp.dot(c_kv, wv_ref[...], preferred_element_type=F32)
    c, sa, sb = c_ref[...], sa_ref[...], sb_ref[...]
    ones_lane = lax.broadcasted_iota(jnp.int32, (1, 2 * LANES), 1) % LANES == V_DIM
    shared = _rope_lanes(jnp.dot(k_rope, wkr_ref[...], preferred_element_type=F32), c, sa, sb)
    for p in range(MLA_HEADS // 2):
        qs, ks = [], []
        for half in range(2):
            cols = slice((2 * p + half) * LANES, (2 * p + half + 1) * LANES)
            qs.append((_rope_lanes(q[:, cols], c, sa, sb) * scale).astype(BF16))
            ks.append((k[:, cols] + shared).astype(BF16))
        q_ref[p] = jnp.concatenate(qs, axis=1)
        k_ref[p] = jnp.concatenate(ks, axis=1)
        v_ref[p] = jnp.where(ones_lane, 1.0, v[:, 2 * p * LANES:2 * (p + 1) * LANES]).astype(BF16)


def _mla_project(x, g, w_in, gq, gkv, wq_pad, wk_pad, wkr, wv, tables, *, tm):
    b, s, d = x.shape
    c, sa, sb = tables
    pairs = MLA_HEADS // 2
    scale = (QK_NOPE + QK_ROPE) ** -0.5 * LOG2_E

    def full(arr):
        return pl.BlockSpec(arr.shape, lambda bi, i: (0,) * arr.ndim)

    tab = pl.BlockSpec((None, tm, LANES), lambda bi, i: (bi, i, 0))
    ops = [g.reshape(1, d), w_in, gq.reshape(1, -1), gkv.reshape(1, -1), wq_pad, wk_pad, wkr, wv]
    return pl.pallas_call(
        functools.partial(_mla_project_kernel, scale=scale),
        grid=(b, s // tm),
        in_specs=[pl.BlockSpec((None, tm, d), lambda bi, i: (bi, i, 0))] + [full(o) for o in ops]
                 + [tab, tab, tab],
        out_specs=[pl.BlockSpec((None, pairs, tm, 2 * LANES), lambda bi, i: (bi, 0, i, 0)),
                   pl.BlockSpec((None, pairs, tm, 2 * LANES), lambda bi, i: (bi, 0, i, 0)),
                   pl.BlockSpec((None, pairs, tm, 2 * LANES), lambda bi, i: (bi, 0, i, 0))],
        out_shape=[jax.ShapeDtypeStruct((b, pairs, s, 2 * LANES), BF16)] * 3,
        compiler_params=_params(("arbitrary", "arbitrary")),
        name="mla_project",
    )(x, *ops, c, sa, sb)


def _flash_kernel(q_ref, k_ref, v_ref, o_ref, s_scr, m_scr, acc_scr, *, tq, tk):
    qi = pl.program_id(2)
    m_scr[...] = jnp.full(m_scr.shape, MASK_VALUE, F32)
    acc_scr[...] = jnp.zeros(acc_scr.shape, F32)
    nt = (((1,), (1,)), ((), ()))

    def key_rows(kj):
        return pl.ds(pl.multiple_of(kj * tk, tk), tk)

    def head_cols(half):
        return slice(half * LANES, (half + 1) * LANES)

    def scores(kj, slot):
        ks = k_ref[key_rows(kj), :]
        for half in range(2):
            s_scr[slot, half] = lax.dot_general(q_ref[:, head_cols(half)], ks[:, head_cols(half)], nt,
                                                preferred_element_type=F32)

    def update(kj, rows, s_pair, triangular):
        n = rows.stop - rows.start
        vs = v_ref[key_rows(kj), :]
        for half in range(2):
            s = s_pair[half]
            if triangular:
                row = lax.broadcasted_iota(jnp.int32, (n, tk), 0)
                col = lax.broadcasted_iota(jnp.int32, (n, tk), 1)
                s = jnp.where(row >= col, s, MASK_VALUE)
            m_prev = m_scr[half, rows]
            m_new = jnp.maximum(m_prev, jnp.max(s, axis=-1, keepdims=True))
            e = jnp.exp2(s - jnp.tile(m_new, (1, tk // LANES)))
            m_scr[half, rows] = m_new
            pv = jnp.dot(e.astype(BF16), vs[:, head_cols(half)], preferred_element_type=F32)
            acc_scr[half, rows] = acc_scr[half, rows] * jnp.exp2(m_prev - m_new) + pv

    everything = slice(0, tq)
    top, bottom = slice(0, tk), slice(tk, tq)

    def from_scratch(slot, rows):
        return [s_scr[slot, half, rows] for half in range(2)]

    scores(0, 0)

    def body(jj, carry):
        scores(2 * jj + 1, 1)
        update(2 * jj, everything, from_scratch(0, everything), False)
        scores(2 * jj + 2, 0)
        update(2 * jj + 1, everything, from_scratch(1, everything), False)
        return carry

    lax.fori_loop(0, qi, body, 0)
    update(2 * qi, top, from_scratch(0, top), True)
    update(2 * qi, bottom, from_scratch(0, bottom), False)
    ks = k_ref[key_rows(2 * qi + 1), :]
    last = [lax.dot_general(q_ref[bottom, head_cols(half)], ks[:, head_cols(half)], nt,
                            preferred_element_type=F32) for half in range(2)]
    update(2 * qi + 1, bottom, last, True)
    out = [acc_scr[half] / acc_scr[half][:, V_DIM:V_DIM + 1] for half in range(2)]
    low = lax.broadcasted_iota(jnp.int32, (tq, LANES), 1) < V_DIM
    o_ref[...] = jnp.where(low, out[0], pltpu.roll(out[1], V_DIM, 1)).astype(o_ref.dtype)


def _flash_attention(q, k, v, *, tk):
    b, pairs, s, _ = q.shape
    tq = 2 * tk
    return pl.pallas_call(
        functools.partial(_flash_kernel, tq=tq, tk=tk),
        grid=(b, pairs, s // tq),
        in_specs=[pl.BlockSpec((None, None, tq, 2 * LANES), lambda bi, p, i: (bi, p, i, 0)),
                  pl.BlockSpec((None, None, s, 2 * LANES), lambda bi, p, i: (bi, p, 0, 0),
                               pipeline_mode=pl.Buffered(1)),
                  pl.BlockSpec((None, None, s, 2 * LANES), lambda bi, p, i: (bi, p, 0, 0),
                               pipeline_mode=pl.Buffered(1))],
        out_specs=pl.BlockSpec((None, tq, LANES), lambda bi, p, i: (bi, i, p)),
        out_shape=jax.ShapeDtypeStruct((b, s, pairs * LANES), BF16),
        scratch_shapes=[pltpu.VMEM((2, 2, tq, tk), F32),
                        pltpu.VMEM((2, tq, LANES), F32), pltpu.VMEM((2, tq, LANES), F32)],
        compiler_params=_params(("arbitrary", "arbitrary", "arbitrary")),
        name="flash_attention",
    )(q, k, v)


def _route_kernel(x_ref, g_ref, wr_ref, meta_ref, h_ref, start_ref, cnt_ref, carry):
    i = pl.program_id(0)
    t = x_ref.shape[0]

    @pl.when(i == 0)
    def _():
        carry[...] = jnp.zeros(carry.shape, F32)

    h = _rms(x_ref[...], g_ref[...])
    h_ref[...] = h
    h_hi = h.astype(BF16)
    h_lo = (h - h_hi.astype(F32)).astype(BF16)
    logits = (jnp.dot(h_hi, wr_ref[0], preferred_element_type=F32)
              + jnp.dot(h_hi, wr_ref[1], preferred_element_type=F32)
              + jnp.dot(h_lo, wr_ref[0], preferred_element_type=F32))
    lane = lax.broadcasted_iota(jnp.int32, (t, LANES), 1)
    lg = jnp.where(lane < N_EXPERTS, logits, -jnp.inf)
    m1 = jnp.max(lg, axis=-1, keepdims=True)
    i1 = jnp.min(jnp.where(lg == m1, lane, LANES), axis=-1, keepdims=True)
    lg2 = jnp.where(lane == i1, -jnp.inf, lg)
    m2 = jnp.max(lg2, axis=-1, keepdims=True)
    i2 = jnp.min(jnp.where(lg2 == m2, lane, LANES), axis=-1, keepdims=True)
    e2 = jnp.exp(m2 - m1)
    w1 = 1.0 / (1.0 + e2)
    w2 = e2 / (1.0 + e2)
    hot1 = lane == i1
    hot2 = lane == i2
    onehot = (hot1 | hot2).astype(BF16)
    r = lax.broadcasted_iota(jnp.int32, (t, t), 0)
    c = lax.broadcasted_iota(jnp.int32, (t, t), 1)
    lower = (c < r).astype(BF16)
    before = jnp.dot(lower, onehot, preferred_element_type=F32) + carry[...]
    start_ref[...] = carry[...]
    rank1 = jnp.sum(jnp.where(hot1, before, 0.0), axis=-1, keepdims=True)
    rank2 = jnp.sum(jnp.where(hot2, before, 0.0), axis=-1, keepdims=True)
    carry[...] += jnp.sum(onehot.astype(F32), axis=0, keepdims=True)
    meta = jnp.zeros((t, LANES), F32)
    for k, val in enumerate((i1.astype(F32), i2.astype(F32), rank1, rank2, w1, w2)):
        meta = jnp.where(lane == k, val, meta)
    meta_ref[...] = meta
    cnt_ref[...] = carry[...]


def _route(x, g, w_router, *, tm):
    m, d = x.shape
    wr = jnp.zeros((d, LANES), F32).at[:, :N_EXPERTS].set(w_router)
    wr_hi = wr.astype(BF16)
    wr = jnp.stack([wr_hi, (wr - wr_hi.astype(F32)).astype(BF16)])
    return pl.pallas_call(
        _route_kernel,
        grid=(m // tm,),
        in_specs=[pl.BlockSpec((tm, d), lambda i: (i, 0)),
                  pl.BlockSpec((1, d), lambda i: (0, 0)),
                  pl.BlockSpec((2, d, LANES), lambda i: (0, 0, 0))],
        out_specs=[pl.BlockSpec((tm, LANES), lambda i: (i, 0)),
                   pl.BlockSpec((tm, d), lambda i: (i, 0)),
                   pl.BlockSpec((None, 1, LANES), lambda i: (i, 0, 0)),
                   pl.BlockSpec((1, LANES), lambda i: (0, 0))],
        out_shape=[jax.ShapeDtypeStruct((m, LANES), F32),
                   jax.ShapeDtypeStruct((m, d), F32),
                   jax.ShapeDtypeStruct((m // tm, 1, LANES), F32),
                   jax.ShapeDtypeStruct((1, LANES), F32)],
        scratch_shapes=[pltpu.VMEM((1, LANES), F32)],
        compiler_params=_params(("arbitrary",)),
        name="route",
    )(x, g.reshape(1, d), wr)


def _invert_kernel(start_ref, base_ref, meta_ref, init_ref, list_ref, obuf, sem):
    del init_ref
    s = pl.program_id(0)
    t = meta_ref.shape[0]
    meta = meta_ref[...]
    second_expert = meta[:, 1:2]
    by_lane = meta.T
    i1, i2, r1, r2 = by_lane[0:1], by_lane[1:2], by_lane[2:3], by_lane[3:4]
    token = lax.broadcasted_iota(jnp.int32, (t, 1), 0) + s * t
    lane = lax.broadcasted_iota(jnp.int32, (t, LANES), 1)
    place = lax.broadcasted_iota(jnp.int32, (t, t), 0).astype(F32)
    copies = []
    for e in range(N_EXPERTS):
        first, second = i1 == e, i2 == e
        start = start_ref[s * N_EXPERTS + e]
        local = jnp.where(first, r1, r2) - start.astype(F32)
        onehot = ((local == place) & (first | second)).astype(BF16)
        code = 2 * token + (second_expert == e).astype(jnp.int32) + 1
        pieces = jnp.where(lane == 0, (code >> 8).astype(F32),
                           jnp.where(lane == 1, (code & 255).astype(F32), 0.0)).astype(BF16)
        obuf[e] = jnp.dot(onehot, pieces, preferred_element_type=F32)
        copy = pltpu.make_async_copy(obuf.at[e], list_ref.at[pl.ds(base_ref[e] + start, t)], sem)
        copy.start()
        copies.append(copy)
    for copy in copies:
        copy.wait()


def _invert(meta, starts, base, *, tm, rows):
    m = meta.shape[0]
    grid_spec = pltpu.PrefetchScalarGridSpec(
        num_scalar_prefetch=2,
        grid=(m // tm,),
        in_specs=[pl.BlockSpec((tm, LANES), lambda i, st, ba: (i, 0)),
                  pl.BlockSpec(memory_space=pl.ANY)],
        out_specs=pl.BlockSpec(memory_space=pl.ANY),
        scratch_shapes=[pltpu.VMEM((N_EXPERTS, tm, LANES), F32), pltpu.SemaphoreType.DMA(())],
    )
    return pl.pallas_call(
        _invert_kernel,
        grid_spec=grid_spec,
        out_shape=jax.ShapeDtypeStruct((rows, LANES), F32),
        input_output_aliases={3: 0},
        compiler_params=_params(("arbitrary",)),
        name="invert",
    )(starts, base, meta, jnp.zeros((rows, LANES), F32))


def _grouped_kernel(te_ref, tv_ref, tb_ref, tp_ref, cur_ref, nxt_ref, dprev_ref, x_hbm, wg_ref, wu_ref, wd_ref,
                    y_hbm, xbuf, hbuf, ybuf, gsem, ssem):
    del te_ref, tb_ref, tp_ref
    i = pl.program_id(0)
    last = pl.num_programs(0) - 1
    tm = xbuf.shape[1]
    slot = i % 2
    other = 1 - slot

    def gather_row(ids_ref, j, buf):
        return pltpu.make_async_copy(x_hbm.at[pl.ds(ids_ref[j], 1)], xbuf.at[buf, pl.ds(j, 1)], gsem.at[buf])

    def scatter_row(j, buf):
        return pltpu.make_async_copy(ybuf.at[buf, pl.ds(j, 1)], y_hbm.at[pl.ds(dprev_ref[j], 1)], ssem.at[buf])

    def gather_looped(ids_ref, buf):
        def body(j, carry):
            gather_row(ids_ref, j, buf).start()
            return carry
        lax.fori_loop(0, tm, body, 0, unroll=8)

    def scatter_looped(buf):
        def body(j, carry):
            scatter_row(j, buf).start()
            return carry
        lax.fori_loop(0, tm, body, 0, unroll=8)

    @pl.when(i == 0)
    def _():
        ybuf[...] = jnp.zeros(ybuf.shape, F32)
        gather_looped(cur_ref, 0)

    pltpu.make_async_copy(x_hbm.at[pl.ds(0, tm)], xbuf.at[slot], gsem.at[slot]).wait()

    @pl.when(i >= 1)
    def _():
        pltpu.make_async_copy(ybuf.at[slot], y_hbm.at[pl.ds(0, tm)], ssem.at[slot]).wait()

    @pl.when(tv_ref[i] > 0)
    def _():
        hbuf[...] = xbuf[slot].astype(BF16)
        for j in range(tm):
            gather_row(nxt_ref, j, other).start(priority=0)
            scatter_row(j, other).start(priority=1)
        h = hbuf[...]
        gate = jnp.dot(h, wg_ref[...], preferred_element_type=F32)
        up = jnp.dot(h, wu_ref[...], preferred_element_type=F32)
        act = (gate * jax.nn.sigmoid(gate) * up).astype(BF16)
        ybuf[slot] = jnp.dot(act, wd_ref[...], preferred_element_type=F32)

    @pl.when(tv_ref[i] == 0)
    def _():
        gather_looped(nxt_ref, other)
        scatter_looped(other)

    @pl.when(i == last)
    def _():
        pltpu.make_async_copy(x_hbm.at[pl.ds(0, tm)], xbuf.at[other], gsem.at[other]).wait()
        pltpu.make_async_copy(ybuf.at[other], y_hbm.at[pl.ds(0, tm)], ssem.at[other]).wait()


def _grouped_swiglu(x, src, dst, tile_expert, tile_valid, tile_block, prev_block, wg, wu, wd, *, tm, y_rows):
    m, d = x.shape
    ff = wg.shape[2]
    n_tiles = tile_expert.shape[0]
    resident = pl.Buffered(1)
    smem_block = functools.partial(pl.BlockSpec, (tm,), memory_space=pltpu.SMEM)
    grid_spec = pltpu.PrefetchScalarGridSpec(
        num_scalar_prefetch=4,
        grid=(n_tiles,),
        in_specs=[smem_block(lambda i, te, tv, tb, tp: (tb[i],)),
                  smem_block(lambda i, te, tv, tb, tp: (tb[jnp.minimum(i + 1, n_tiles - 1)],)),
                  smem_block(lambda i, te, tv, tb, tp: (tp[i],)),
                  pl.BlockSpec(memory_space=pl.ANY),
                  pl.BlockSpec((None, d, ff), lambda i, te, tv, tb, tp: (te[i], 0, 0), pipeline_mode=resident),
                  pl.BlockSpec((None, d, ff), lambda i, te, tv, tb, tp: (te[i], 0, 0), pipeline_mode=resident),
                  pl.BlockSpec((None, ff, d), lambda i, te, tv, tb, tp: (te[i], 0, 0), pipeline_mode=resident)],
        out_specs=pl.BlockSpec(memory_space=pl.ANY),
        scratch_shapes=[pltpu.VMEM((2, tm, d), F32), pltpu.VMEM((tm, d), BF16), pltpu.VMEM((2, tm, d), F32),
                        pltpu.SemaphoreType.DMA((2,)), pltpu.SemaphoreType.DMA((2,))],
    )
    return pl.pallas_call(
        _grouped_kernel,
        grid_spec=grid_spec,
        out_shape=jax.ShapeDtypeStruct((y_rows, d), F32),
        compiler_params=_params(("arbitrary",), vmem=GROUPED_VMEM_LIMIT),
        name="grouped_swiglu",
    )(tile_expert, tile_valid, tile_block, prev_block, src, src, dst, x, wg, wu, wd)


def _combine_kernel(x_ref, meta_ref, g_ref, y0_ref, y1_ref, o_ref):
    meta = meta_ref[...]
    x = x_ref[...] + (meta[:, 4:5] * y0_ref[...] + meta[:, 5:6] * y1_ref[...])
    o_ref[...] = _rms(x, g_ref[...])


def _combine(x, meta, y, g, *, tm):
    m, d = x.shape
    return pl.pallas_call(
        _combine_kernel,
        grid=(m // tm,),
        in_specs=[pl.BlockSpec((tm, d), lambda i: (i, 0)),
                  pl.BlockSpec((tm, LANES), lambda i: (i, 0)),
                  pl.BlockSpec((1, d), lambda i: (0, 0)),
                  pl.BlockSpec((tm, d), lambda i: (i, 0)),
                  pl.BlockSpec((tm, d), lambda i: (m // tm + i, 0))],
        out_specs=pl.BlockSpec((tm, d), lambda i: (i, 0)),
        out_shape=jax.ShapeDtypeStruct((m, d), F32),
        compiler_params=_params(("arbitrary",)),
        name="combine",
    )(x, meta, g.reshape(1, d), y, y)


def _moe_layer(x, g, w_router, wg, wu, wd, final_g, *, tm):
    m, d = x.shape
    route_tm = 512
    meta, hn, starts, counts = _route(x, g, w_router, tm=route_tm)
    cnt = counts[0, :N_EXPERTS].astype(jnp.int32)
    tiles = (cnt + tm - 1) // tm
    tile_end = jnp.cumsum(tiles)
    first_tile = tile_end - tiles
    n_tiles = (2 * m) // tm + N_EXPERTS
    tile_id = jnp.arange(n_tiles, dtype=jnp.int32)
    tile_expert = jnp.sum(tile_id[:, None] >= tile_end[None, :], axis=1).astype(jnp.int32)
    tile_valid = (tile_expert < N_EXPERTS).astype(jnp.int32)
    tile_expert = jnp.minimum(tile_expert, N_EXPERTS - 1)
    assert route_tm <= tm
    base = (first_tile + jnp.arange(N_EXPERTS, dtype=jnp.int32)) * tm
    list_blocks = n_tiles + N_EXPERTS
    lists = _invert(meta, starts[:, 0, :N_EXPERTS].astype(jnp.int32).reshape(-1), base.astype(jnp.int32),
                    tm=route_tm, rows=list_blocks * tm)
    code = (lists[:, 0] * 256.0 + lists[:, 1]).astype(jnp.int32) - 1
    src = jnp.maximum(code, 0) >> 1
    spare = 2 * m + jnp.arange(tm, dtype=jnp.int32)
    dst = jnp.where(code < 0, jnp.tile(spare, list_blocks), (code & 1) * m + (code >> 1))
    empty_block = list_blocks - 1
    tile_block = tile_id + tile_expert
    dst_block = jnp.where(tile_valid > 0, tile_block, empty_block).astype(jnp.int32)
    prev_block = jnp.concatenate([jnp.full((1,), empty_block, jnp.int32), dst_block[:-1]])
    tile_block = jnp.where(tile_valid > 0, tile_block, empty_block).astype(jnp.int32)
    y = _grouped_swiglu(hn, src, dst, tile_expert, tile_valid, tile_block, prev_block, wg, wu, wd, tm=tm,
                        y_rows=2 * m + tm)
    return _combine(x, meta, y, final_g, tm=512)


def _mla_weights(w_q_up, w_kv_up):
    hd = QK_NOPE + QK_ROPE
    wq = w_q_up.reshape(Q_LORA, MLA_HEADS, hd)
    wq_pad = jnp.pad(wq, ((0, 0), (0, 0), (0, LANES - hd))).reshape(Q_LORA, MLA_HEADS * LANES)
    wkv = w_kv_up.reshape(KV_LORA, MLA_HEADS, QK_NOPE + V_DIM)
    wk_pad = jnp.pad(wkv[:, :, :QK_NOPE], ((0, 0), (0, 0), (0, LANES - QK_NOPE)))
    wk_pad = wk_pad.reshape(KV_LORA, MLA_HEADS * LANES)
    wv = jnp.pad(wkv[:, :, QK_NOPE:], ((0, 0), (0, 0), (0, LANES - V_DIM))).reshape(KV_LORA, MLA_HEADS * LANES)
    place = jnp.zeros((QK_ROPE, LANES), F32).at[jnp.arange(QK_ROPE), QK_NOPE + jnp.arange(QK_ROPE)].set(1.0)
    return wq_pad.astype(BF16), wk_pad.astype(BF16), place.astype(BF16), wv.astype(BF16)


def _rope_lane_tables(positions):
    half = QK_ROPE // 2
    inv = ROPE_THETA ** (-jnp.arange(0, QK_ROPE, 2, dtype=F32) / QK_ROPE)
    ang = positions.astype(F32)[..., None] * inv
    cos, sin = jnp.cos(ang), jnp.sin(ang)
    shape = positions.shape
    ones = jnp.ones(shape + (QK_NOPE,), F32)
    zeros_n = jnp.zeros(shape + (QK_NOPE,), F32)
    zeros_h = jnp.zeros(shape + (half,), F32)
    pad = jnp.zeros(shape + (LANES - QK_NOPE - QK_ROPE,), F32)
    c = jnp.concatenate([ones, cos, cos, pad], axis=-1)
    sa = jnp.concatenate([zeros_n, -sin, zeros_h, pad], axis=-1)
    sb = jnp.concatenate([zeros_n, zeros_h, sin, pad], axis=-1)
    return c, sa, sb


def kernel(x, positions, mix_norm_g, ffn_norm_g, a_w_qkv, a_w_out, b_w_in, b_q_norm_g, b_kv_norm_g,
           b_w_q_up, b_w_kv_up, b_w_out, ffn_w_gate, ffn_w_up, ffn_w_down, moe_w_router, moe_w_gate,
           moe_w_up, moe_w_down, final_norm_g):
    b, s, d = x.shape
    m = b * s
    xf = x.reshape(m, d)

    group_cols = 3 * DSA_WIDTH
    q_scale = jnp.where(jnp.arange(a_w_qkv.shape[2]) % group_cols < DSA_WIDTH, DSA_HEAD_DIM ** -0.5, 1.0)
    w_qkv = (a_w_qkv[0] * q_scale).astype(BF16)
    state = None
    for group, (_, dil) in enumerate(DIL_PAIRS):
        w_group = w_qkv[:, group * group_cols:(group + 1) * group_cols]
        qkv_v = _qkv_project(xf, mix_norm_g[0], w_group, dil=dil, tm=1024 if dil == 1 else 512)
        qkv_v = qkv_v.reshape(b, s // dil, dil * group_cols)
        state = _merge_group(qkv_v, positions, state, group=group, qb=4)
    x1 = _unpermute_project(xf, state, a_w_out[0].astype(BF16), dil=DIL_PAIRS[-1][1], tm=512)
    x2 = _swiglu(x1, ffn_norm_g[0], ffn_w_gate[0].astype(BF16), ffn_w_up[0].astype(BF16),
                 ffn_w_down[0].astype(BF16), tm=512)

    wq_pad, wk_pad, wkr, wv = _mla_weights(b_w_q_up[0], b_w_kv_up[0])
    q, k, v = _mla_project(x2.reshape(b, s, d), mix_norm_g[1], b_w_in[0].astype(BF16), b_q_norm_g[0],
                           b_kv_norm_g[0], wq_pad, wk_pad, wkr, wv, _rope_lane_tables(positions), tm=512)
    attn = _flash_attention(q, k, v, tk=512)
    x3 = _matmul_residual(x2, attn.reshape(m, MLA_HEADS * V_DIM), b_w_out[0].astype(BF16), tm=1024)
    out = _moe_layer(x3, ffn_norm_g[1], moe_w_router[0], moe_w_gate[0].astype(BF16),
                     moe_w_up[0].astype(BF16), moe_w_down[0].astype(BF16), final_norm_g,
                     tm=512)
    return out.reshape(b, s, d)
```

```python
import functools

import jax
import jax.numpy as jnp
from jax import lax
from jax.experimental import pallas as pl
from jax.experimental.pallas import tpu as pltpu

F32 = jnp.float32
BF16 = jnp.bfloat16

RMS_EPS = 1e-6
BLOCK = 128
LANES = 128
DIL_PAIRS = ((128, 1), (512, 4), (2048, 16))
DSA_HEADS = 8
DSA_HEAD_DIM = 64
DSA_WIDTH = DSA_HEADS * DSA_HEAD_DIM
MLA_HEADS = 16
QK_NOPE = 64
QK_ROPE = 32
V_DIM = 64
Q_LORA = 384
KV_LORA = 256
ROPE_THETA = 10000.0
N_EXPERTS = 8
MASK_VALUE = -1e30
LOG2_E = 1.4426950408889634
VMEM_LIMIT = 56 << 20
GROUPED_VMEM_LIMIT = 60 << 20


def _params(semantics, vmem=VMEM_LIMIT):
    return pltpu.CompilerParams(dimension_semantics=semantics, vmem_limit_bytes=vmem)


def _rms(x, g):
    ms = jnp.mean(x * x, axis=-1, keepdims=True)
    return x * lax.rsqrt(ms + RMS_EPS) * g


def _residue_slot(dil):
    if dil == 16:
        return tuple(4 * (r % 4) + r // 4 for r in range(dil))
    return tuple(range(dil))


def _deinterleave(tm, dil):
    n = tm // dil
    out_row = jnp.arange(tm, dtype=jnp.int32)
    source = (out_row % n) * dil + out_row // n
    return (source[:, None] == out_row[None, :]).astype(BF16)


def _qkv_kernel(x_ref, g_ref, perm_ref, w_ref, o_ref, *, dil):
    tm = x_ref.shape[0]
    n = tm // dil
    width = w_ref.shape[1]
    h = _rms(x_ref[...], g_ref[...]).astype(BF16)
    if dil > 1:
        h = jnp.dot(perm_ref[...], h, preferred_element_type=F32).astype(BF16)
    res = jnp.dot(h, w_ref[...], preferred_element_type=F32)
    for r, slot in enumerate(_residue_slot(dil)):
        o_ref[:, slot * width:(slot + 1) * width] = res[r * n:(r + 1) * n, :].astype(o_ref.dtype)


def _qkv_project(x, g, w, *, dil, tm):
    m, d = x.shape
    width = w.shape[1]
    n = tm // dil
    perm = _deinterleave(tm, dil)
    return pl.pallas_call(
        functools.partial(_qkv_kernel, dil=dil),
        grid=(m // tm,),
        in_specs=[pl.BlockSpec((tm, d), lambda i: (i, 0)),
                  pl.BlockSpec((1, d), lambda i: (0, 0)),
                  pl.BlockSpec((tm, tm), lambda i: (0, 0)),
                  pl.BlockSpec((d, width), lambda i: (0, 0))],
        out_specs=pl.BlockSpec((n, dil * width), lambda i: (i, 0)),
        out_shape=jax.ShapeDtypeStruct((m // dil, dil * width), BF16),
        compiler_params=_params(("arbitrary",)),
        name=f"qkv_project_d{dil}",
    )(x, g.reshape(1, d), perm, w)


def _matmul_residual_kernel(x_ref, a_ref, w_ref, o_ref):
    o_ref[...] = x_ref[...] + jnp.dot(a_ref[...], w_ref[...], preferred_element_type=F32)


def _matmul_residual(x, a, w, *, tm):
    m, d = x.shape
    k = a.shape[1]
    return pl.pallas_call(
        _matmul_residual_kernel,
        grid=(m // tm,),
        in_specs=[pl.BlockSpec((tm, d), lambda i: (i, 0)),
                  pl.BlockSpec((tm, k), lambda i: (i, 0)),
                  pl.BlockSpec((k, d), lambda i: (0, 0))],
        out_specs=pl.BlockSpec((tm, d), lambda i: (i, 0)),
        out_shape=jax.ShapeDtypeStruct((m, d), F32),
        compiler_params=_params(("arbitrary",)),
        name="matmul_residual",
    )(x, a, w)


def _merge_kernel(*refs, qb, span, has_prev, emit_lse):
    q_ref, k_ref, v_ref, prow_ref = refs[:4]
    pos = 4
    if has_prev:
        oprev_ref, lprev_ref = refs[pos:pos + 2]
        pos += 2
    if emit_lse:
        perm_ref, o_ref, lse_ref = refs[pos:pos + 3]
        pos += 3
    else:
        o_ref = refs[pos]
        pos += 1
    kbuf, vbuf, pbuf = refs[pos:pos + 3]
    if emit_lse:
        o_scr, lse_scr = refs[pos + 3:pos + 5]

    n = pl.program_id(2)
    tq = qb * BLOCK

    @pl.when(n == 0)
    def _():
        kbuf[0:BLOCK, :] = jnp.zeros((BLOCK, DSA_WIDTH), BF16)
        vbuf[0:BLOCK, :] = jnp.zeros((BLOCK, DSA_WIDTH), BF16)
        pbuf[:, 0:BLOCK] = jnp.zeros((1, BLOCK), jnp.int32)

    kbuf[BLOCK:, :] = k_ref[...]
    vbuf[BLOCK:, :] = v_ref[...]
    pbuf[:, BLOCK:] = prow_ref[...]

    lane = lax.broadcasted_iota(jnp.int32, (BLOCK, LANES), 1)
    sub = lax.broadcasted_iota(jnp.int32, (BLOCK, LANES), 0)
    low = lane < DSA_HEAD_DIM
    diag = lane == sub
    row = lax.broadcasted_iota(jnp.int32, (BLOCK, 2 * BLOCK), 0)
    col = lax.broadcasted_iota(jnp.int32, (BLOCK, 2 * BLOCK), 1)
    rel = row + BLOCK - col
    band = (rel >= 0) & (rel <= span)
    first_valid = band & ((col >= BLOCK) | (n > 0))
    spread = (lax.broadcasted_iota(jnp.int32, (LANES, DSA_WIDTH), 0)
              == lax.broadcasted_iota(jnp.int32, (LANES, DSA_WIDTH), 1) // DSA_HEAD_DIM).astype(BF16)

    def per_head(c):
        hi = c.astype(BF16)
        lo = (c - hi.astype(F32)).astype(BF16)
        return (jnp.dot(hi, spread, preferred_element_type=F32)
                + jnp.dot(lo, spread, preferred_element_type=F32))

    def lane_to_column(v):
        return jnp.sum(jnp.where(diag, v, 0.0), axis=1, keepdims=True)

    for i in range(qb):
        rows = slice(i * BLOCK, (i + 1) * BLOCK)
        keys = slice(i * BLOCK, (i + 2) * BLOCK)
        kk = kbuf[keys, :]
        vv = vbuf[keys, :]
        pk = pbuf[:, keys]
        pq_row = pbuf[:, (i + 1) * BLOCK:(i + 2) * BLOCK]
        pq = ((lane_to_column((pq_row >> 12).astype(F32)).astype(jnp.int32) << 12)
              | lane_to_column((pq_row & 4095).astype(F32)).astype(jnp.int32))
        dist = jnp.abs(pq - pk).astype(F32)
        neg_dist = jnp.where(first_valid if i == 0 else band, -dist, MASK_VALUE)
        q = q_ref[rows, :]
        m_tile = jnp.zeros((BLOCK, LANES), F32)
        l_tile = jnp.ones((BLOCK, LANES), F32)
        pv_pairs = []
        for p in range(DSA_HEADS // 2):
            cols = slice(p * LANES, (p + 1) * LANES)
            q2, k2, v2 = q[:, cols], kk[:, cols], vv[:, cols]
            pvs = []
            for half in range(2):
                h = 2 * p + half
                slope = 2.0 ** (-8.0 * (h + 1) / DSA_HEADS)
                qm = jnp.where(low if half == 0 else ~low, q2, jnp.zeros_like(q2))
                s = lax.dot_general(qm, k2, (((1,), (1,)), ((), ())), preferred_element_type=F32)
                s = s + slope * neg_dist
                m = jnp.max(s, axis=-1, keepdims=True)
                e = jnp.exp(s - m)
                m_tile = jnp.where(lane == h, m, m_tile)
                l_tile = jnp.where(lane == h, jnp.sum(e, axis=-1, keepdims=True), l_tile)
                pvs.append(jnp.dot(e.astype(BF16), v2, preferred_element_type=F32))
            pv_pairs.append(jnp.where(low, pvs[0], pvs[1]))
        pv_all = jnp.concatenate(pv_pairs, axis=1)
        lse = m_tile + jnp.log(l_tile)
        if has_prev:
            lp = lprev_ref[rows, :]
            mx = jnp.maximum(lp, lse)
            wa = jnp.exp(lp - mx)
            wb = jnp.exp(lse - mx)
            den = wa + wb
            c_new = wb / (den * l_tile)
            lse = mx + jnp.log(den)
            out = pv_all * per_head(c_new) + oprev_ref[rows, :].astype(F32) * per_head(wa / den)
        else:
            out = pv_all * per_head(1.0 / l_tile)
        if emit_lse:
            o_scr[rows, :] = out.astype(BF16)
            lse_scr[rows, :] = lse
        else:
            o_ref[rows, :] = out.astype(o_ref.dtype)

    if emit_lse:
        part = tq // 4
        o_perm = jnp.dot(perm_ref[...], o_scr[...], preferred_element_type=F32)
        for r in range(4):
            o_ref[:, r * DSA_WIDTH:(r + 1) * DSA_WIDTH] = o_perm[r * part:(r + 1) * part, :].astype(o_ref.dtype)
            lse_ref[:, r * LANES:(r + 1) * LANES] = lse_scr[pl.ds(r, part, stride=4), :]

    tail = slice(qb * BLOCK, (qb + 1) * BLOCK)
    kbuf[0:BLOCK, :] = kbuf[tail, :]
    vbuf[0:BLOCK, :] = vbuf[tail, :]
    pbuf[:, 0:BLOCK] = pbuf[:, tail]


def _merge_group(qkv_v, positions, prev, *, group, qb):
    b, s = positions.shape
    window, dil = DIL_PAIRS[group]
    last = group == len(DIL_PAIRS) - 1
    span = window // dil
    sub = s // dil
    tq = qb * BLOCK
    nb = sub // tq
    residue_of_slot = sorted(range(dil), key=_residue_slot(dil).__getitem__)
    prow = positions.reshape(b, sub, dil).transpose(0, 2, 1)[:, jnp.asarray(residue_of_slot), :]
    prow = prow.reshape(b, dil, 1, sub)

    def qkv_spec(which):
        return pl.BlockSpec((None, tq, DSA_WIDTH), lambda bi, r, n: (bi, n, r * 3 + which))

    in_specs = [qkv_spec(0), qkv_spec(1), qkv_spec(2),
                pl.BlockSpec((None, None, 1, tq), lambda bi, r, n: (bi, r, 0, n))]
    args = [qkv_v, qkv_v, qkv_v, prow]
    if prev is not None:
        in_specs += [pl.BlockSpec((None, tq, DSA_WIDTH), lambda bi, r, n: (bi, n, r)),
                     pl.BlockSpec((None, tq, LANES), lambda bi, r, n: (bi, n, r))]
        args += list(prev)
    scratch = [pltpu.VMEM((tq + BLOCK, DSA_WIDTH), BF16), pltpu.VMEM((tq + BLOCK, DSA_WIDTH), BF16),
               pltpu.VMEM((1, tq + BLOCK), jnp.int32)]
    if last:
        out_specs = [pl.BlockSpec((None, tq, DSA_WIDTH), lambda bi, r, n: (bi, n, r))]
        out_shape = [jax.ShapeDtypeStruct((b, sub, dil * DSA_WIDTH), BF16)]
    else:
        in_specs.append(pl.BlockSpec((tq, tq), lambda bi, r, n: (0, 0)))
        args.append(_deinterleave(tq, 4))
        out_specs = [pl.BlockSpec((None, tq // 4, 4 * DSA_WIDTH), lambda bi, r, n: (bi, n, r)),
                     pl.BlockSpec((None, tq // 4, 4 * LANES), lambda bi, r, n: (bi, n, r))]
        out_shape = [jax.ShapeDtypeStruct((b, sub // 4, 4 * dil * DSA_WIDTH), BF16),
                     jax.ShapeDtypeStruct((b, sub // 4, 4 * dil * LANES), F32)]
        scratch += [pltpu.VMEM((tq, DSA_WIDTH), BF16), pltpu.VMEM((tq, LANES), F32)]
    outs = pl.pallas_call(
        functools.partial(_merge_kernel, qb=qb, span=span, has_prev=prev is not None, emit_lse=not last),
        grid=(b, dil, nb),
        in_specs=in_specs,
        out_specs=out_specs,
        out_shape=out_shape,
        scratch_shapes=scratch,
        compiler_params=_params(("arbitrary", "arbitrary", "arbitrary")),
        name=f"dilated_group{group}",
    )(*args)
    return outs[0] if last else tuple(outs)


def _unpermute_project_kernel(x_ref, a_ref, perm_ref, w_ref, o_ref, *, dil):
    width = w_ref.shape[0]
    stacked = jnp.concatenate([a_ref[:, slot * width:(slot + 1) * width] for slot in _residue_slot(dil)],
                              axis=0)
    tokens = jnp.dot(perm_ref[...], stacked, preferred_element_type=F32).astype(BF16)
    o_ref[...] = x_ref[...] + jnp.dot(tokens, w_ref[...], preferred_element_type=F32)


def _unpermute_project(x, a, w, *, dil, tm):
    m, d = x.shape
    width = w.shape[0]
    return pl.pallas_call(
        functools.partial(_unpermute_project_kernel, dil=dil),
        grid=(m // tm,),
        in_specs=[pl.BlockSpec((tm, d), lambda i: (i, 0)),
                  pl.BlockSpec((tm // dil, dil * width), lambda i: (i, 0)),
                  pl.BlockSpec((tm, tm), lambda i: (0, 0)),
                  pl.BlockSpec((width, d), lambda i: (0, 0))],
        out_specs=pl.BlockSpec((tm, d), lambda i: (i, 0)),
        out_shape=jax.ShapeDtypeStruct((m, d), F32),
        compiler_params=_params(("arbitrary",)),
        name="unpermute_project",
    )(x, a.reshape(m // dil, dil * width), _deinterleave(tm, dil).T, w)


def _swiglu_kernel(x_ref, g_ref, wg_ref, wu_ref, wd_ref, o_ref):
    x = x_ref[...]
    h = _rms(x, g_ref[...]).astype(BF16)
    gate = jnp.dot(h, wg_ref[...], preferred_element_type=F32)
    up = jnp.dot(h, wu_ref[...], preferred_element_type=F32)
    act = (gate * jax.nn.sigmoid(gate) * up).astype(BF16)
    o_ref[...] = x + jnp.dot(act, wd_ref[...], preferred_element_type=F32)


def _swiglu(x, g, wg, wu, wd, *, tm):
    m, d = x.shape
    ff = wg.shape[1]
    resident = pl.Buffered(1)
    return pl.pallas_call(
        _swiglu_kernel,
        grid=(m // tm,),
        in_specs=[pl.BlockSpec((tm, d), lambda i: (i, 0)),
                  pl.BlockSpec((1, d), lambda i: (0, 0)),
                  pl.BlockSpec((d, ff), lambda i: (0, 0), pipeline_mode=resident),
                  pl.BlockSpec((d, ff), lambda i: (0, 0), pipeline_mode=resident),
                  pl.BlockSpec((ff, d), lambda i: (0, 0), pipeline_mode=resident)],
        out_specs=pl.BlockSpec((tm, d), lambda i: (i, 0)),
        out_shape=jax.ShapeDtypeStruct((m, d), F32),
        compiler_params=_params(("arbitrary",)),
        name="swiglu",
    )(x, g.reshape(1, d), wg, wu, wd)


def _rope_lanes(x, c, sa, sb):
    return x * c + pltpu.roll(x, LANES - QK_ROPE // 2, 1) * sa + pltpu.roll(x, QK_ROPE // 2, 1) * sb


def _mla_project_kernel(x_ref, g_ref, win_ref, gq_ref, gkv_ref, wq_ref, wk_ref, wkr_ref, wv_ref,
                        c_ref, sa_ref, sb_ref, q_ref, k_ref, v_ref, *, scale):
    h = _rms(x_ref[...], g_ref[...]).astype(BF16)
    a = jnp.dot(h, win_ref[...], preferred_element_type=F32)
    c_q = _rms(a[:, :Q_LORA], gq_ref[...]).astype(BF16)
    c_kv = _rms(a[:, Q_LORA:Q_LORA + KV_LORA], gkv_ref[...]).astype(BF16)
    k_rope = a[:, Q_LORA + KV_LORA:].astype(BF16)
    q = jnp.dot(c_q, wq_ref[...], preferred_element_type=F32)
    k = jnp.dot(c_kv, wk_ref[...], preferred_element_type=F32)
    v = jnp.dot(c_kv, wv_ref[...], preferred_element_type=F32)
    c, sa, sb = c_ref[...], sa_ref[...], sb_ref[...]
    ones_lane = lax.broadcasted_iota(jnp.int32, (1, 2 * LANES), 1) % LANES == V_DIM
    shared = _rope_lanes(jnp.dot(k_rope, wkr_ref[...], preferred_element_type=F32), c, sa, sb)
    for p in range(MLA_HEADS // 2):
        qs, ks = [], []
        for half in range(2):
            cols = slice((2 * p + half) * LANES, (2 * p + half + 1) * LANES)
            qs.append((_rope_lanes(q[:, cols], c, sa, sb) * scale).astype(BF16))
            ks.append((k[:, cols] + shared).astype(BF16))
        q_ref[p] = jnp.concatenate(qs, axis=1)
        k_ref[p] = jnp.concatenate(ks, axis=1)
        v_ref[p] = jnp.where(ones_lane, 1.0, v[:, 2 * p * LANES:2 * (p + 1) * LANES]).astype(BF16)


def _mla_project(x, g, w_in, gq, gkv, wq_pad, wk_pad, wkr, wv, tables, *, tm):
    b, s, d = x.shape
    c, sa, sb = tables
    pairs = MLA_HEADS // 2
    scale = (QK_NOPE + QK_ROPE) ** -0.5 * LOG2_E

    def full(arr):
        return pl.BlockSpec(arr.shape, lambda bi, i: (0,) * arr.ndim)

    tab = pl.BlockSpec((None, tm, LANES), lambda bi, i: (bi, i, 0))
    ops = [g.reshape(1, d), w_in, gq.reshape(1, -1), gkv.reshape(1, -1), wq_pad, wk_pad, wkr, wv]
    return pl.pallas_call(
        functools.partial(_mla_project_kernel, scale=scale),
        grid=(b, s // tm),
        in_specs=[pl.BlockSpec((None, tm, d), lambda bi, i: (bi, i, 0))] + [full(o) for o in ops]
                 + [tab, tab, tab],
        out_specs=[pl.BlockSpec((None, pairs, tm, 2 * LANES), lambda bi, i: (bi, 0, i, 0)),
                   pl.BlockSpec((None, pairs, tm, 2 * LANES), lambda bi, i: (bi, 0, i, 0)),
                   pl.BlockSpec((None, pairs, tm, 2 * LANES), lambda bi, i: (bi, 0, i, 0))],
        out_shape=[jax.ShapeDtypeStruct((b, pairs, s, 2 * LANES), BF16)] * 3,
        compiler_params=_params(("arbitrary", "arbitrary")),
        name="mla_project",
    )(x, *ops, c, sa, sb)


def _flash_kernel(q_ref, k_ref, v_ref, o_ref, s_scr, m_scr, acc_scr, *, tq, tk):
    qi = pl.program_id(2)
    m_scr[...] = jnp.full(m_scr.shape, MASK_VALUE, F32)
    acc_scr[...] = jnp.zeros(acc_scr.shape, F32)
    nt = (((1,), (1,)), ((), ()))

    def key_rows(kj):
        return pl.ds(pl.multiple_of(kj * tk, tk), tk)

    def head_cols(half):
        return slice(half * LANES, (half + 1) * LANES)

    def scores(kj, slot):
        ks = k_ref[key_rows(kj), :]
        for half in range(2):
            s_scr[slot, half] = lax.dot_general(q_ref[:, head_cols(half)], ks[:, head_cols(half)], nt,
                                                preferred_element_type=F32)

    def update(kj, rows, s_pair, triangular):
        n = rows.stop - rows.start
        vs = v_ref[key_rows(kj), :]
        for half in range(2):
            s = s_pair[half]
            if triangular:
                row = lax.broadcasted_iota(jnp.int32, (n, tk), 0)
                col = lax.broadcasted_iota(jnp.int32, (n, tk), 1)
                s = jnp.where(row >= col, s, MASK_VALUE)
            m_prev = m_scr[half, rows]
            m_new = jnp.maximum(m_prev, jnp.max(s, axis=-1, keepdims=True))
            e = jnp.exp2(s - jnp.tile(m_new, (1, tk // LANES)))
            m_scr[half, rows] = m_new
            pv = jnp.dot(e.astype(BF16), vs[:, head_cols(half)], preferred_element_type=F32)
            acc_scr[half, rows] = acc_scr[half, rows] * jnp.exp2(m_prev - m_new) + pv

    everything = slice(0, tq)
    top, bottom = slice(0, tk), slice(tk, tq)

    def from_scratch(slot, rows):
        return [s_scr[slot, half, rows] for half in range(2)]

    scores(0, 0)

    def body(jj, carry):
        scores(2 * jj + 1, 1)
        update(2 * jj, everything, from_scratch(0, everything), False)
        scores(2 * jj + 2, 0)
        update(2 * jj + 1, everything, from_scratch(1, everything), False)
        return carry

    lax.fori_loop(0, qi, body, 0)
    update(2 * qi, top, from_scratch(0, top), True)
    update(2 * qi, bottom, from_scratch(0, bottom), False)
    ks = k_ref[key_rows(2 * qi + 1), :]
    last = [lax.dot_general(q_ref[bottom, head_cols(half)], ks[:, head_cols(half)], nt,
                            preferred_element_type=F32) for half in range(2)]
    update(2 * qi + 1, bottom, last, True)
    out = [acc_scr[half] / acc_scr[half][:, V_DIM:V_DIM + 1] for half in range(2)]
    low = lax.broadcasted_iota(jnp.int32, (tq, LANES), 1) < V_DIM
    o_ref[...] = jnp.where(low, out[0], pltpu.roll(out[1], V_DIM, 1)).astype(o_ref.dtype)


def _flash_attention(q, k, v, *, tk):
    b, pairs, s, _ = q.shape
    tq = 2 * tk
    return pl.pallas_call(
        functools.partial(_flash_kernel, tq=tq, tk=tk),
        grid=(b, pairs, s // tq),
        in_specs=[pl.BlockSpec((None, None, tq, 2 * LANES), lambda bi, p, i: (bi, p, i, 0)),
                  pl.BlockSpec((None, None, s, 2 * LANES), lambda bi, p, i: (bi, p, 0, 0),
                               pipeline_mode=pl.Buffered(1)),
                  pl.BlockSpec((None, None, s, 2 * LANES), lambda bi, p, i: (bi, p, 0, 0),
                               pipeline_mode=pl.Buffered(1))],
        out_specs=pl.BlockSpec((None, tq, LANES), lambda bi, p, i: (bi, i, p)),
        out_shape=jax.ShapeDtypeStruct((b, s, pairs * LANES), BF16),
        scratch_shapes=[pltpu.VMEM((2, 2, tq, tk), F32),
                        pltpu.VMEM((2, tq, LANES), F32), pltpu.VMEM((2, tq, LANES), F32)],
        compiler_params=_params(("arbitrary", "arbitrary", "arbitrary")),
        name="flash_attention",
    )(q, k, v)


def _route_kernel(x_ref, g_ref, wr_ref, meta_ref, h_ref, start_ref, cnt_ref, carry):
    i = pl.program_id(0)
    t = x_ref.shape[0]

    @pl.when(i == 0)
    def _():
        carry[...] = jnp.zeros(carry.shape, F32)

    h = _rms(x_ref[...], g_ref[...])
    h_ref[...] = h
    h_hi = h.astype(BF16)
    h_lo = (h - h_hi.astype(F32)).astype(BF16)
    logits = (jnp.dot(h_hi, wr_ref[0], preferred_element_type=F32)
              + jnp.dot(h_hi, wr_ref[1], preferred_element_type=F32)
              + jnp.dot(h_lo, wr_ref[0], preferred_element_type=F32))
    lane = lax.broadcasted_iota(jnp.int32, (t, LANES), 1)
    lg = jnp.where(lane < N_EXPERTS, logits, -jnp.inf)
    m1 = jnp.max(lg, axis=-1, keepdims=True)
    i1 = jnp.min(jnp.where(lg == m1, lane, LANES), axis=-1, keepdims=True)
    lg2 = jnp.where(lane == i1, -jnp.inf, lg)
    m2 = jnp.max(lg2, axis=-1, keepdims=True)
    i2 = jnp.min(jnp.where(lg2 == m2, lane, LANES), axis=-1, keepdims=True)
    e2 = jnp.exp(m2 - m1)
    w1 = 1.0 / (1.0 + e2)
    w2 = e2 / (1.0 + e2)
    hot1 = lane == i1
    hot2 = lane == i2
    onehot = (hot1 | hot2).astype(BF16)
    r = lax.broadcasted_iota(jnp.int32, (t, t), 0)
    c = lax.broadcasted_iota(jnp.int32, (t, t), 1)
    lower = (c < r).astype(BF16)
    before = jnp.dot(lower, onehot, preferred_element_type=F32) + carry[...]
    start_ref[...] = carry[...]
    rank1 = jnp.sum(jnp.where(hot1, before, 0.0), axis=-1, keepdims=True)
    rank2 = jnp.sum(jnp.where(hot2, before, 0.0), axis=-1, keepdims=True)
    carry[...] += jnp.sum(onehot.astype(F32), axis=0, keepdims=True)
    meta = jnp.zeros((t, LANES), F32)
    for k, val in enumerate((i1.astype(F32), i2.astype(F32), rank1, rank2, w1, w2)):
        meta = jnp.where(lane == k, val, meta)
    meta_ref[...] = meta
    cnt_ref[...] = carry[...]


def _route(x, g, w_router, *, tm):
    m, d = x.shape
    wr = jnp.zeros((d, LANES), F32).at[:, :N_EXPERTS].set(w_router)
    wr_hi = wr.astype(BF16)
    wr = jnp.stack([wr_hi, (wr - wr_hi.astype(F32)).astype(BF16)])
    return pl.pallas_call(
        _route_kernel,
        grid=(m // tm,),
        in_specs=[pl.BlockSpec((tm, d), lambda i: (i, 0)),
                  pl.BlockSpec((1, d), lambda i: (0, 0)),
                  pl.BlockSpec((2, d, LANES), lambda i: (0, 0, 0))],
        out_specs=[pl.BlockSpec((tm, LANES), lambda i: (i, 0)),
                   pl.BlockSpec((tm, d), lambda i: (i, 0)),
                   pl.BlockSpec((None, 1, LANES), lambda i: (i, 0, 0)),
                   pl.BlockSpec((1, LANES), lambda i: (0, 0))],
        out_shape=[jax.ShapeDtypeStruct((m, LANES), F32),
                   jax.ShapeDtypeStruct((m, d), F32),
                   jax.ShapeDtypeStruct((m // tm, 1, LANES), F32),
                   jax.ShapeDtypeStruct((1, LANES), F32)],
        scratch_shapes=[pltpu.VMEM((1, LANES), F32)],
        compiler_params=_params(("arbitrary",)),
        name="route",
    )(x, g.reshape(1, d), wr)


def _invert_kernel(start_ref, base_ref, meta_ref, init_ref, list_ref, obuf, sem):
    del init_ref
    s = pl.program_id(0)
    t = meta_ref.shape[0]
    meta = meta_ref[...]
    second_expert = meta[:, 1:2]
    by_lane = meta.T
    i1, i2, r1, r2 = by_lane[0:1], by_lane[1:2], by_lane[2:3], by_lane[3:4]
    token = lax.broadcasted_iota(jnp.int32, (t, 1), 0) + s * t
    lane = lax.broadcasted_iota(jnp.int32, (t, LANES), 1)
    place = lax.broadcasted_iota(jnp.int32, (t, t), 0).astype(F32)
    copies = []
    for e in range(N_EXPERTS):
        first, second = i1 == e, i2 == e
        start = start_ref[s * N_EXPERTS + e]
        local = jnp.where(first, r1, r2) - start.astype(F32)
        onehot = ((local == place) & (first | second)).astype(BF16)
        code = 2 * token + (second_expert == e).astype(jnp.int32) + 1
        pieces = jnp.where(lane == 0, (code >> 8).astype(F32),
                           jnp.where(lane == 1, (code & 255).astype(F32), 0.0)).astype(BF16)
        obuf[e] = jnp.dot(onehot, pieces, preferred_element_type=F32)
        copy = pltpu.make_async_copy(obuf.at[e], list_ref.at[pl.ds(base_ref[e] + start, t)], sem)
        copy.start()
        copies.append(copy)
    for copy in copies:
        copy.wait()


def _invert(meta, starts, base, *, tm, rows):
    m = meta.shape[0]
    grid_spec = pltpu.PrefetchScalarGridSpec(
        num_scalar_prefetch=2,
        grid=(m // tm,),
        in_specs=[pl.BlockSpec((tm, LANES), lambda i, st, ba: (i, 0)),
                  pl.BlockSpec(memory_space=pl.ANY)],
        out_specs=pl.BlockSpec(memory_space=pl.ANY),
        scratch_shapes=[pltpu.VMEM((N_EXPERTS, tm, LANES), F32), pltpu.SemaphoreType.DMA(())],
    )
    return pl.pallas_call(
        _invert_kernel,
        grid_spec=grid_spec,
        out_shape=jax.ShapeDtypeStruct((rows, LANES), F32),
        input_output_aliases={3: 0},
        compiler_params=_params(("arbitrary",)),
        name="invert",
    )(starts, base, meta, jnp.zeros((rows, LANES), F32))


def _grouped_kernel(te_ref, tv_ref, tb_ref, tp_ref, cur_ref, nxt_ref, dprev_ref, x_hbm, wg_ref, wu_ref, wd_ref,
                    y_hbm, xbuf, hbuf, ybuf, gsem, ssem):
    del te_ref, tb_ref, tp_ref
    i = pl.program_id(0)
    last = pl.num_programs(0) - 1
    tm = xbuf.shape[1]
    slot = i % 2
    other = 1 - slot

    def gather_row(ids_ref, j, buf):
        return pltpu.make_async_copy(x_hbm.at[pl.ds(ids_ref[j], 1)], xbuf.at[buf, pl.ds(j, 1)], gsem.at[buf])

    def scatter_row(j, buf):
        return pltpu.make_async_copy(ybuf.at[buf, pl.ds(j, 1)], y_hbm.at[pl.ds(dprev_ref[j], 1)], ssem.at[buf])

    def gather_looped(ids_ref, buf):
        def body(j, carry):
            gather_row(ids_ref, j, buf).start()
            return carry
        lax.fori_loop(0, tm, body, 0, unroll=8)

    def scatter_looped(buf):
        def body(j, carry):
            scatter_row(j, buf).start()
            return carry
        lax.fori_loop(0, tm, body, 0, unroll=8)

    @pl.when(i == 0)
    def _():
        ybuf[...] = jnp.zeros(ybuf.shape, F32)
        gather_looped(cur_ref, 0)

    pltpu.make_async_copy(x_hbm.at[pl.ds(0, tm)], xbuf.at[slot], gsem.at[slot]).wait()

    @pl.when(i >= 1)
    def _():
        pltpu.make_async_copy(ybuf.at[slot], y_hbm.at[pl.ds(0, tm)], ssem.at[slot]).wait()

    @pl.when(tv_ref[i] > 0)
    def _():
        hbuf[...] = xbuf[slot].astype(BF16)
        for j in range(tm):
            gather_row(nxt_ref, j, other).start(priority=0)
            scatter_row(j, other).start(priority=1)
        h = hbuf[...]
        gate = jnp.dot(h, wg_ref[...], preferred_element_type=F32)
        up = jnp.dot(h, wu_ref[...], preferred_element_type=F32)
        act = (gate * jax.nn.sigmoid(gate) * up).astype(BF16)
        ybuf[slot] = jnp.dot(act, wd_ref[...], preferred_element_type=F32)

    @pl.when(tv_ref[i] == 0)
    def _():
        gather_looped(nxt_ref, other)
        scatter_looped(other)

    @pl.when(i == last)
    def _():
        pltpu.make_async_copy(x_hbm.at[pl.ds(0, tm)], xbuf.at[other], gsem.at[other]).wait()
        pltpu.make_async_copy(ybuf.at[other], y_hbm.at[pl.ds(0, tm)], ssem.at[other]).wait()


def _grouped_swiglu(x, src, dst, tile_expert, tile_valid, tile_block, prev_block, wg, wu, wd, *, tm, y_rows):
    m, d = x.shape
    ff = wg.shape[2]
    n_tiles = tile_expert.shape[0]
    resident = pl.Buffered(1)
    smem_block = functools.partial(pl.BlockSpec, (tm,), memory_space=pltpu.SMEM)
    grid_spec = pltpu.PrefetchScalarGridSpec(
        num_scalar_prefetch=4,
        grid=(n_tiles,),
        in_specs=[smem_block(lambda i, te, tv, tb, tp: (tb[i],)),
                  smem_block(lambda i, te, tv, tb, tp: (tb[jnp.minimum(i + 1, n_tiles - 1)],)),
                  smem_block(lambda i, te, tv, tb, tp: (tp[i],)),
                  pl.BlockSpec(memory_space=pl.ANY),
                  pl.BlockSpec((None, d, ff), lambda i, te, tv, tb, tp: (te[i], 0, 0), pipeline_mode=resident),
                  pl.BlockSpec((None, d, ff), lambda i, te, tv, tb, tp: (te[i], 0, 0), pipeline_mode=resident),
                  pl.BlockSpec((None, ff, d), lambda i, te, tv, tb, tp: (te[i], 0, 0), pipeline_mode=resident)],
        out_specs=pl.BlockSpec(memory_space=pl.ANY),
        scratch_shapes=[pltpu.VMEM((2, tm, d), F32), pltpu.VMEM((tm, d), BF16), pltpu.VMEM((2, tm, d), F32),
                        pltpu.SemaphoreType.DMA((2,)), pltpu.SemaphoreType.DMA((2,))],
    )
    return pl.pallas_call(
        _grouped_kernel,
        grid_spec=grid_spec,
        out_shape=jax.ShapeDtypeStruct((y_rows, d), F32),
        compiler_params=_params(("arbitrary",), vmem=GROUPED_VMEM_LIMIT),
        name="grouped_swiglu",
    )(tile_expert, tile_valid, tile_block, prev_block, src, src, dst, x, wg, wu, wd)


def _combine_kernel(x_ref, meta_ref, g_ref, y0_ref, y1_ref, o_ref):
    meta = meta_ref[...]
    x = x_ref[...] + (meta[:, 4:5] * y0_ref[...] + meta[:, 5:6] * y1_ref[...])
    o_ref[...] = _rms(x, g_ref[...])


def _combine(x, meta, y, g, *, tm):
    m, d = x.shape
    return pl.pallas_call(
        _combine_kernel,
        grid=(m // tm,),
        in_specs=[pl.BlockSpec((tm, d), lambda i: (i, 0)),
                  pl.BlockSpec((tm, LANES), lambda i: (i, 0)),
                  pl.BlockSpec((1, d), lambda i: (0, 0)),
                  pl.BlockSpec((tm, d), lambda i: (i, 0)),
                  pl.BlockSpec((tm, d), lambda i: (m // tm + i, 0))],
        out_specs=pl.BlockSpec((tm, d), lambda i: (i, 0)),
        out_shape=jax.ShapeDtypeStruct((m, d), F32),
        compiler_params=_params(("arbitrary",)),
        name="combine",
    )(x, meta, g.reshape(1, d), y, y)


def _moe_layer(x, g, w_router, wg, wu, wd, final_g, *, tm):
    m, d = x.shape
    route_tm = 512
    meta, hn, starts, counts = _route(x, g, w_router, tm=route_tm)
    cnt = counts[0, :N_EXPERTS].astype(jnp.int32)
    tiles = (cnt + tm - 1) // tm
    tile_end = jnp.cumsum(tiles)
    first_tile = tile_end - tiles
    n_tiles = (2 * m) // tm + N_EXPERTS
    tile_id = jnp.arange(n_tiles, dtype=jnp.int32)
    tile_expert = jnp.sum(tile_id[:, None] >= tile_end[None, :], axis=1).astype(jnp.int32)
    tile_valid = (tile_expert < N_EXPERTS).astype(jnp.int32)
    tile_expert = jnp.minimum(tile_expert, N_EXPERTS - 1)
    assert route_tm <= tm
    base = (first_tile + jnp.arange(N_EXPERTS, dtype=jnp.int32)) * tm
    list_blocks = n_tiles + N_EXPERTS
    lists = _invert(meta, starts[:, 0, :N_EXPERTS].astype(jnp.int32).reshape(-1), base.astype(jnp.int32),
                    tm=route_tm, rows=list_blocks * tm)
    code = (lists[:, 0] * 256.0 + lists[:, 1]).astype(jnp.int32) - 1
    row_in_tile = jnp.tile(jnp.arange(tm, dtype=jnp.int32), list_blocks)
    src = jnp.where(code < 0, row_in_tile, code >> 1)
    dst = jnp.where(code < 0, 2 * m + row_in_tile, (code & 1) * m + (code >> 1))
    empty_block = list_blocks - 1
    tile_block = tile_id + tile_expert
    dst_block = jnp.where(tile_valid > 0, tile_block, empty_block).astype(jnp.int32)
    prev_block = jnp.concatenate([jnp.full((1,), empty_block, jnp.int32), dst_block[:-1]])
    tile_block = jnp.where(tile_valid > 0, tile_block, empty_block).astype(jnp.int32)
    y = _grouped_swiglu(hn, src, dst, tile_expert, tile_valid, tile_block, prev_block, wg, wu, wd, tm=tm,
                        y_rows=2 * m + tm)
    return _combine(x, meta, y, final_g, tm=512)


def _mla_weights(w_q_up, w_kv_up):
    hd = QK_NOPE + QK_ROPE
    wq = w_q_up.reshape(Q_LORA, MLA_HEADS, hd)
    wq_pad = jnp.pad(wq, ((0, 0), (0, 0), (0, LANES - hd))).reshape(Q_LORA, MLA_HEADS * LANES)
    wkv = w_kv_up.reshape(KV_LORA, MLA_HEADS, QK_NOPE + V_DIM)
    wk_pad = jnp.pad(wkv[:, :, :QK_NOPE], ((0, 0), (0, 0), (0, LANES - QK_NOPE)))
    wk_pad = wk_pad.reshape(KV_LORA, MLA_HEADS * LANES)
    wv = jnp.pad(wkv[:, :, QK_NOPE:], ((0, 0), (0, 0), (0, LANES - V_DIM))).reshape(KV_LORA, MLA_HEADS * LANES)
    place = jnp.zeros((QK_ROPE, LANES), F32).at[jnp.arange(QK_ROPE), QK_NOPE + jnp.arange(QK_ROPE)].set(1.0)
    return wq_pad.astype(BF16), wk_pad.astype(BF16), place.astype(BF16), wv.astype(BF16)


def _rope_lane_tables(positions):
    half = QK_ROPE // 2
    inv = ROPE_THETA ** (-jnp.arange(0, QK_ROPE, 2, dtype=F32) / QK_ROPE)
    ang = positions.astype(F32)[..., None] * inv
    cos, sin = jnp.cos(ang), jnp.sin(ang)
    shape = positions.shape
    ones = jnp.ones(shape + (QK_NOPE,), F32)
    zeros_n = jnp.zeros(shape + (QK_NOPE,), F32)
    zeros_h = jnp.zeros(shape + (half,), F32)
    pad = jnp.zeros(shape + (LANES - QK_NOPE - QK_ROPE,), F32)
    c = jnp.concatenate([ones, cos, cos, pad], axis=-1)
    sa = jnp.concatenate([zeros_n, -sin, zeros_h, pad], axis=-1)
    sb = jnp.concatenate([zeros_n, zeros_h, sin, pad], axis=-1)
    return c, sa, sb


def kernel(x, positions, mix_norm_g, ffn_norm_g, a_w_qkv, a_w_out, b_w_in, b_q_norm_g, b_kv_norm_g,
           b_w_q_up, b_w_kv_up, b_w_out, ffn_w_gate, ffn_w_up, ffn_w_down, moe_w_router, moe_w_gate,
           moe_w_up, moe_w_down, final_norm_g):
    b, s, d = x.shape
    m = b * s
    xf = x.reshape(m, d)

    group_cols = 3 * DSA_WIDTH
    q_scale = jnp.where(jnp.arange(a_w_qkv.shape[2]) % group_cols < DSA_WIDTH, DSA_HEAD_DIM ** -0.5, 1.0)
    w_qkv = (a_w_qkv[0] * q_scale).astype(BF16)
    state = None
    for group, (_, dil) in enumerate(DIL_PAIRS):
        w_group = w_qkv[:, group * group_cols:(group + 1) * group_cols]
        qkv_v = _qkv_project(xf, mix_norm_g[0], w_group, dil=dil, tm=1024 if dil == 1 else 512)
        qkv_v = qkv_v.reshape(b, s // dil, dil * group_cols)
        state = _merge_group(qkv_v, positions, state, group=group, qb=4)
    x1 = _unpermute_project(xf, state, a_w_out[0].astype(BF16), dil=DIL_PAIRS[-1][1], tm=512)
    x2 = _swiglu(x1, ffn_norm_g[0], ffn_w_gate[0].astype(BF16), ffn_w_up[0].astype(BF16),
                 ffn_w_down[0].astype(BF16), tm=512)

    wq_pad, wk_pad, wkr, wv = _mla_weights(b_w_q_up[0], b_w_kv_up[0])
    q, k, v = _mla_project(x2.reshape(b, s, d), mix_norm_g[1], b_w_in[0].astype(BF16), b_q_norm_g[0],
                           b_kv_norm_g[0], wq_pad, wk_pad, wkr, wv, _rope_lane_tables(positions), tm=512)
    attn = _flash_attention(q, k, v, tk=512)
    x3 = _matmul_residual(x2, attn.reshape(m, MLA_HEADS * V_DIM), b_w_out[0].astype(BF16), tm=1024)
    out = _moe_layer(x3, ffn_norm_g[1], moe_w_router[0], moe_w_gate[0].astype(BF16),
                     moe_w_up[0].astype(BF16), moe_w_down[0].astype(BF16), final_norm_g,
                     tm=512)
    return out.reshape(b, s, d)
```

```python
import functools

import jax
import jax.numpy as jnp
from jax import lax
from jax.experimental import pallas as pl
from jax.experimental.pallas import tpu as pltpu

F32 = jnp.float32
BF16 = jnp.bfloat16

RMS_EPS = 1e-6
BLOCK = 128
LANES = 128
DIL_PAIRS = ((128, 1), (512, 4), (2048, 16))
DSA_HEADS = 8
DSA_HEAD_DIM = 64
DSA_WIDTH = DSA_HEADS * DSA_HEAD_DIM
MLA_HEADS = 16
QK_NOPE = 64
QK_ROPE = 32
V_DIM = 64
Q_LORA = 384
KV_LORA = 256
ROPE_THETA = 10000.0
N_EXPERTS = 8
MASK_VALUE = -1e30
LOG2_E = 1.4426950408889634
VMEM_LIMIT = 56 << 20
GROUPED_VMEM_LIMIT = 60 << 20


def _params(semantics, vmem=VMEM_LIMIT):
    return pltpu.CompilerParams(dimension_semantics=semantics, vmem_limit_bytes=vmem)


def _rms(x, g):
    ms = jnp.mean(x * x, axis=-1, keepdims=True)
    return x * lax.rsqrt(ms + RMS_EPS) * g


def _residue_slot(dil):
    if dil == 16:
        return tuple(4 * (r % 4) + r // 4 for r in range(dil))
    return tuple(range(dil))


def _deinterleave(tm, dil):
    n = tm // dil
    out_row = jnp.arange(tm, dtype=jnp.int32)
    source = (out_row % n) * dil + out_row // n
    return (source[:, None] == out_row[None, :]).astype(BF16)


def _qkv_kernel(x_ref, g_ref, perm_ref, w_ref, o_ref, *, dil):
    tm = x_ref.shape[0]
    n = tm // dil
    width = w_ref.shape[1]
    h = _rms(x_ref[...], g_ref[...]).astype(BF16)
    if dil > 1:
        h = jnp.dot(perm_ref[...], h, preferred_element_type=F32).astype(BF16)
    res = jnp.dot(h, w_ref[...], preferred_element_type=F32)
    for r, slot in enumerate(_residue_slot(dil)):
        o_ref[:, slot * width:(slot + 1) * width] = res[r * n:(r + 1) * n, :].astype(o_ref.dtype)


def _qkv_project(x, g, w, *, dil, tm):
    m, d = x.shape
    width = w.shape[1]
    n = tm // dil
    perm = _deinterleave(tm, dil)
    return pl.pallas_call(
        functools.partial(_qkv_kernel, dil=dil),
        grid=(m // tm,),
        in_specs=[pl.BlockSpec((tm, d), lambda i: (i, 0)),
                  pl.BlockSpec((1, d), lambda i: (0, 0)),
                  pl.BlockSpec((tm, tm), lambda i: (0, 0)),
                  pl.BlockSpec((d, width), lambda i: (0, 0))],
        out_specs=pl.BlockSpec((n, dil * width), lambda i: (i, 0)),
        out_shape=jax.ShapeDtypeStruct((m // dil, dil * width), BF16),
        compiler_params=_params(("arbitrary",)),
        name=f"qkv_project_d{dil}",
    )(x, g.reshape(1, d), perm, w)


def _matmul_residual_kernel(x_ref, a_ref, w_ref, o_ref):
    o_ref[...] = x_ref[...] + jnp.dot(a_ref[...], w_ref[...], preferred_element_type=F32)


def _matmul_residual(x, a, w, *, tm):
    m, d = x.shape
    k = a.shape[1]
    return pl.pallas_call(
        _matmul_residual_kernel,
        grid=(m // tm,),
        in_specs=[pl.BlockSpec((tm, d), lambda i: (i, 0)),
                  pl.BlockSpec((tm, k), lambda i: (i, 0)),
                  pl.BlockSpec((k, d), lambda i: (0, 0))],
        out_specs=pl.BlockSpec((tm, d), lambda i: (i, 0)),
        out_shape=jax.ShapeDtypeStruct((m, d), F32),
        compiler_params=_params(("arbitrary",)),
        name="matmul_residual",
    )(x, a, w)


def _merge_kernel(*refs, qb, span, has_prev, emit_lse):
    q_ref, k_ref, v_ref, prow_ref = refs[:4]
    pos = 4
    if has_prev:
        oprev_ref, lprev_ref = refs[pos:pos + 2]
        pos += 2
    if emit_lse:
        perm_ref, o_ref, lse_ref = refs[pos:pos + 3]
        pos += 3
    else:
        o_ref = refs[pos]
        pos += 1
    kbuf, vbuf, pbuf = refs[pos:pos + 3]
    if emit_lse:
        o_scr, lse_scr = refs[pos + 3:pos + 5]

    n = pl.program_id(2)
    tq = qb * BLOCK

    @pl.when(n == 0)
    def _():
        kbuf[0:BLOCK, :] = jnp.zeros((BLOCK, DSA_WIDTH), BF16)
        vbuf[0:BLOCK, :] = jnp.zeros((BLOCK, DSA_WIDTH), BF16)
        pbuf[:, 0:BLOCK] = jnp.zeros((1, BLOCK), jnp.int32)

    kbuf[BLOCK:, :] = k_ref[...]
    vbuf[BLOCK:, :] = v_ref[...]
    pbuf[:, BLOCK:] = prow_ref[...]

    lane = lax.broadcasted_iota(jnp.int32, (BLOCK, LANES), 1)
    sub = lax.broadcasted_iota(jnp.int32, (BLOCK, LANES), 0)
    low = lane < DSA_HEAD_DIM
    diag = lane == sub
    row = lax.broadcasted_iota(jnp.int32, (BLOCK, 2 * BLOCK), 0)
    col = lax.broadcasted_iota(jnp.int32, (BLOCK, 2 * BLOCK), 1)
    rel = row + BLOCK - col
    band = (rel >= 0) & (rel <= span)
    first_valid = band & ((col >= BLOCK) | (n > 0))
    spread = (lax.broadcasted_iota(jnp.int32, (LANES, DSA_WIDTH), 0)
              == lax.broadcasted_iota(jnp.int32, (LANES, DSA_WIDTH), 1) // DSA_HEAD_DIM).astype(BF16)

    def per_head(c):
        hi = c.astype(BF16)
        lo = (c - hi.astype(F32)).astype(BF16)
        return (jnp.dot(hi, spread, preferred_element_type=F32)
                + jnp.dot(lo, spread, preferred_element_type=F32))

    def lane_to_column(v):
        return jnp.sum(jnp.where(diag, v, 0.0), axis=1, keepdims=True)

    for i in range(qb):
        rows = slice(i * BLOCK, (i + 1) * BLOCK)
        keys = slice(i * BLOCK, (i + 2) * BLOCK)
        kk = kbuf[keys, :]
        vv = vbuf[keys, :]
        pk = pbuf[:, keys]
        pq_row = pbuf[:, (i + 1) * BLOCK:(i + 2) * BLOCK]
        pq = ((lane_to_column((pq_row >> 12).astype(F32)).astype(jnp.int32) << 12)
              | lane_to_column((pq_row & 4095).astype(F32)).astype(jnp.int32))
        dist = jnp.abs(pq - pk).astype(F32)
        neg_dist = jnp.where(first_valid if i == 0 else band, -dist, MASK_VALUE)
        q = q_ref[rows, :]
        m_tile = jnp.zeros((BLOCK, LANES), F32)
        l_tile = jnp.ones((BLOCK, LANES), F32)
        pv_pairs = []
        for p in range(DSA_HEADS // 2):
            cols = slice(p * LANES, (p + 1) * LANES)
            q2, k2, v2 = q[:, cols], kk[:, cols], vv[:, cols]
            pvs = []
            for half in range(2):
                h = 2 * p + half
                slope = 2.0 ** (-8.0 * (h + 1) / DSA_HEADS)
                qm = jnp.where(low if half == 0 else ~low, q2, jnp.zeros_like(q2))
                s = lax.dot_general(qm, k2, (((1,), (1,)), ((), ())), preferred_element_type=F32)
                s = s + slope * neg_dist
                m = jnp.max(s, axis=-1, keepdims=True)
                e = jnp.exp(s - m)
                m_tile = jnp.where(lane == h, m, m_tile)
                l_tile = jnp.where(lane == h, jnp.sum(e, axis=-1, keepdims=True), l_tile)
                pvs.append(jnp.dot(e.astype(BF16), v2, preferred_element_type=F32))
            pv_pairs.append(jnp.where(low, pvs[0], pvs[1]))
        pv_all = jnp.concatenate(pv_pairs, axis=1)
        lse = m_tile + jnp.log(l_tile)
        if has_prev:
            lp = lprev_ref[rows, :]
            mx = jnp.maximum(lp, lse)
            wa = jnp.exp(lp - mx)
            wb = jnp.exp(lse - mx)
            den = wa + wb
            c_new = wb / (den * l_tile)
            lse = mx + jnp.log(den)
            out = pv_all * per_head(c_new) + oprev_ref[rows, :].astype(F32) * per_head(wa / den)
        else:
            out = pv_all * per_head(1.0 / l_tile)
        if emit_lse:
            o_scr[rows, :] = out.astype(BF16)
            lse_scr[rows, :] = lse
        else:
            o_ref[rows, :] = out.astype(o_ref.dtype)

    if emit_lse:
        part = tq // 4
        o_perm = jnp.dot(perm_ref[...], o_scr[...], preferred_element_type=F32)
        for r in range(4):
            o_ref[:, r * DSA_WIDTH:(r + 1) * DSA_WIDTH] = o_perm[r * part:(r + 1) * part, :].astype(o_ref.dtype)
            lse_ref[:, r * LANES:(r + 1) * LANES] = lse_scr[pl.ds(r, part, stride=4), :]

    tail = slice(qb * BLOCK, (qb + 1) * BLOCK)
    kbuf[0:BLOCK, :] = kbuf[tail, :]
    vbuf[0:BLOCK, :] = vbuf[tail, :]
    pbuf[:, 0:BLOCK] = pbuf[:, tail]


def _merge_group(qkv_v, positions, prev, *, group, qb):
    b, s = positions.shape
    window, dil = DIL_PAIRS[group]
    last = group == len(DIL_PAIRS) - 1
    span = window // dil
    sub = s // dil
    tq = qb * BLOCK
    nb = sub // tq
    residue_of_slot = sorted(range(dil), key=_residue_slot(dil).__getitem__)
    prow = positions.reshape(b, sub, dil).transpose(0, 2, 1)[:, jnp.asarray(residue_of_slot), :]
    prow = prow.reshape(b, dil, 1, sub)

    def qkv_spec(which):
        return pl.BlockSpec((None, tq, DSA_WIDTH), lambda bi, r, n: (bi, n, r * 3 + which))

    in_specs = [qkv_spec(0), qkv_spec(1), qkv_spec(2),
                pl.BlockSpec((None, None, 1, tq), lambda bi, r, n: (bi, r, 0, n))]
    args = [qkv_v, qkv_v, qkv_v, prow]
    if prev is not None:
        in_specs += [pl.BlockSpec((None, tq, DSA_WIDTH), lambda bi, r, n: (bi, n, r)),
                     pl.BlockSpec((None, tq, LANES), lambda bi, r, n: (bi, n, r))]
        args += list(prev)
    scratch = [pltpu.VMEM((tq + BLOCK, DSA_WIDTH), BF16), pltpu.VMEM((tq + BLOCK, DSA_WIDTH), BF16),
               pltpu.VMEM((1, tq + BLOCK), jnp.int32)]
    if last:
        out_specs = [pl.BlockSpec((None, tq, DSA_WIDTH), lambda bi, r, n: (bi, n, r))]
        out_shape = [jax.ShapeDtypeStruct((b, sub, dil * DSA_WIDTH), BF16)]
    else:
        in_specs.append(pl.BlockSpec((tq, tq), lambda bi, r, n: (0, 0)))
        args.append(_deinterleave(tq, 4))
        out_specs = [pl.BlockSpec((None, tq // 4, 4 * DSA_WIDTH), lambda bi, r, n: (bi, n, r)),
                     pl.BlockSpec((None, tq // 4, 4 * LANES), lambda bi, r, n: (bi, n, r))]
        out_shape = [jax.ShapeDtypeStruct((b, sub // 4, 4 * dil * DSA_WIDTH), BF16),
                     jax.ShapeDtypeStruct((b, sub // 4, 4 * dil * LANES), F32)]
        scratch += [pltpu.VMEM((tq, DSA_WIDTH), BF16), pltpu.VMEM((tq, LANES), F32)]
    outs = pl.pallas_call(
        functools.partial(_merge_kernel, qb=qb, span=span, has_prev=prev is not None, emit_lse=not last),
        grid=(b, dil, nb),
        in_specs=in_specs,
        out_specs=out_specs,
        out_shape=out_shape,
        scratch_shapes=scratch,
        compiler_params=_params(("arbitrary", "arbitrary", "arbitrary")),
        name=f"dilated_group{group}",
    )(*args)
    return outs[0] if last else tuple(outs)


def _unpermute_project_kernel(x_ref, a_ref, perm_ref, w_ref, o_ref, *, dil):
    width = w_ref.shape[0]
    stacked = jnp.concatenate([a_ref[:, slot * width:(slot + 1) * width] for slot in _residue_slot(dil)],
                              axis=0)
    tokens = jnp.dot(perm_ref[...], stacked, preferred_element_type=F32).astype(BF16)
    o_ref[...] = x_ref[...] + jnp.dot(tokens, w_ref[...], preferred_element_type=F32)


def _unpermute_project(x, a, w, *, dil, tm):
    m, d = x.shape
    width = w.shape[0]
    return pl.pallas_call(
        functools.partial(_unpermute_project_kernel, dil=dil),
        grid=(m // tm,),
        in_specs=[pl.BlockSpec((tm, d), lambda i: (i, 0)),
                  pl.BlockSpec((tm // dil, dil * width), lambda i: (i, 0)),
                  pl.BlockSpec((tm, tm), lambda i: (0, 0)),
                  pl.BlockSpec((width, d), lambda i: (0, 0))],
        out_specs=pl.BlockSpec((tm, d), lambda i: (i, 0)),
        out_shape=jax.ShapeDtypeStruct((m, d), F32),
        compiler_params=_params(("arbitrary",)),
        name="unpermute_project",
    )(x, a.reshape(m // dil, dil * width), _deinterleave(tm, dil).T, w)


def _swiglu_kernel(x_ref, g_ref, wg_ref, wu_ref, wd_ref, o_ref):
    x = x_ref[...]
    h = _rms(x, g_ref[...]).astype(BF16)
    gate = jnp.dot(h, wg_ref[...], preferred_element_type=F32)
    up = jnp.dot(h, wu_ref[...], preferred_element_type=F32)
    act = (gate * jax.nn.sigmoid(gate) * up).astype(BF16)
    o_ref[...] = x + jnp.dot(act, wd_ref[...], preferred_element_type=F32)


def _swiglu(x, g, wg, wu, wd, *, tm):
    m, d = x.shape
    ff = wg.shape[1]
    resident = pl.Buffered(1)
    return pl.pallas_call(
        _swiglu_kernel,
        grid=(m // tm,),
        in_specs=[pl.BlockSpec((tm, d), lambda i: (i, 0)),
                  pl.BlockSpec((1, d), lambda i: (0, 0)),
                  pl.BlockSpec((d, ff), lambda i: (0, 0), pipeline_mode=resident),
                  pl.BlockSpec((d, ff), lambda i: (0, 0), pipeline_mode=resident),
                  pl.BlockSpec((ff, d), lambda i: (0, 0), pipeline_mode=resident)],
        out_specs=pl.BlockSpec((tm, d), lambda i: (i, 0)),
        out_shape=jax.ShapeDtypeStruct((m, d), F32),
        compiler_params=_params(("arbitrary",)),
        name="swiglu",
    )(x, g.reshape(1, d), wg, wu, wd)


def _rope_lanes(x, c, sa, sb):
    return x * c + pltpu.roll(x, LANES - QK_ROPE // 2, 1) * sa + pltpu.roll(x, QK_ROPE // 2, 1) * sb


def _mla_project_kernel(x_ref, g_ref, win_ref, gq_ref, gkv_ref, wq_ref, wk_ref, wkr_ref, wv_ref,
                        c_ref, sa_ref, sb_ref, q_ref, k_ref, v_ref, *, scale):
    h = _rms(x_ref[...], g_ref[...]).astype(BF16)
    a = jnp.dot(h, win_ref[...], preferred_element_type=F32)
    c_q = _rms(a[:, :Q_LORA], gq_ref[...]).astype(BF16)
    c_kv = _rms(a[:, Q_LORA:Q_LORA + KV_LORA], gkv_ref[...]).astype(BF16)
    k_rope = a[:, Q_LORA + KV_LORA:].astype(BF16)
    q = jnp.dot(c_q, wq_ref[...], preferred_element_type=F32)
    k = jnp.dot(c_kv, wk_ref[...], preferred_element_type=F32)
    v = jnp.dot(c_kv, wv_ref[...], preferred_element_type=F32)
    c, sa, sb = c_ref[...], sa_ref[...], sb_ref[...]
    ones_lane = lax.broadcasted_iota(jnp.int32, (1, 2 * LANES), 1) % LANES == V_DIM
    shared = _rope_lanes(jnp.dot(k_rope, wkr_ref[...], preferred_element_type=F32), c, sa, sb)
    for p in range(MLA_HEADS // 2):
        qs, ks = [], []
        for half in range(2):
            cols = slice((2 * p + half) * LANES, (2 * p + half + 1) * LANES)
            qs.append((_rope_lanes(q[:, cols], c, sa, sb) * scale).astype(BF16))
            ks.append((k[:, cols] + shared).astype(BF16))
        q_ref[p] = jnp.concatenate(qs, axis=1)
        k_ref[p] = jnp.concatenate(ks, axis=1)
        v_ref[p] = jnp.where(ones_lane, 1.0, v[:, 2 * p * LANES:2 * (p + 1) * LANES]).astype(BF16)


def _mla_project(x, g, w_in, gq, gkv, wq_pad, wk_pad, wkr, wv, tables, *, tm):
    b, s, d = x.shape
    c, sa, sb = tables
    pairs = MLA_HEADS // 2
    scale = (QK_NOPE + QK_ROPE) ** -0.5 * LOG2_E

    def full(arr):
        return pl.BlockSpec(arr.shape, lambda bi, i: (0,) * arr.ndim)

    tab = pl.BlockSpec((None, tm, LANES), lambda bi, i: (bi, i, 0))
    ops = [g.reshape(1, d), w_in, gq.reshape(1, -1), gkv.reshape(1, -1), wq_pad, wk_pad, wkr, wv]
    return pl.pallas_call(
        functools.partial(_mla_project_kernel, scale=scale),
        grid=(b, s // tm),
        in_specs=[pl.BlockSpec((None, tm, d), lambda bi, i: (bi, i, 0))] + [full(o) for o in ops]
                 + [tab, tab, tab],
        out_specs=[pl.BlockSpec((None, pairs, tm, 2 * LANES), lambda bi, i: (bi, 0, i, 0)),
                   pl.BlockSpec((None, pairs, tm, 2 * LANES), lambda bi, i: (bi, 0, i, 0)),
                   pl.BlockSpec((None, pairs, tm, 2 * LANES), lambda bi, i: (bi, 0, i, 0))],
        out_shape=[jax.ShapeDtypeStruct((b, pairs, s, 2 * LANES), BF16)] * 3,
        compiler_params=_params(("arbitrary", "arbitrary")),
        name="mla_project",
    )(x, *ops, c, sa, sb)


def _flash_kernel(q_ref, k_ref, v_ref, o_ref, s_scr, m_scr, acc_scr, *, tq, tk):
    nq = q_ref.shape[0] // tq
    nt = (((1,), (1,)), ((), ()))

    def query_rows(qi, rows):
        return pl.ds(pl.multiple_of(qi * tq, tq) + rows.start, rows.stop - rows.start)

    def key_rows(kj):
        return pl.ds(pl.multiple_of(kj * tk, tk), tk)

    def head_cols(half):
        return slice(half * LANES, (half + 1) * LANES)

    everything = slice(0, tq)
    top, bottom = slice(0, tk), slice(tk, tq)

    def scores(qi, kj, slot):
        ks = k_ref[key_rows(kj), :]
        for half in range(2):
            s_scr[slot, half] = lax.dot_general(q_ref[query_rows(qi, everything), head_cols(half)],
                                                ks[:, head_cols(half)], nt, preferred_element_type=F32)

    def update(kj, rows, s_pair, triangular):
        n = rows.stop - rows.start
        vs = v_ref[key_rows(kj), :]
        for half in range(2):
            s = s_pair[half]
            if triangular:
                row = lax.broadcasted_iota(jnp.int32, (n, tk), 0)
                col = lax.broadcasted_iota(jnp.int32, (n, tk), 1)
                s = jnp.where(row >= col, s, MASK_VALUE)
            m_prev = m_scr[half, rows]
            m_new = jnp.maximum(m_prev, jnp.max(s, axis=-1, keepdims=True))
            e = jnp.exp2(s - jnp.tile(m_new, (1, tk // LANES)))
            m_scr[half, rows] = m_new
            pv = jnp.dot(e.astype(BF16), vs[:, head_cols(half)], preferred_element_type=F32)
            acc_scr[half, rows] = acc_scr[half, rows] * jnp.exp2(m_prev - m_new) + pv

    def from_scratch(slot, rows):
        return [s_scr[slot, half, rows] for half in range(2)]

    def reset():
        m_scr[...] = jnp.full(m_scr.shape, MASK_VALUE, F32)
        acc_scr[...] = jnp.zeros(acc_scr.shape, F32)

    reset()
    scores(0, 0, 0)

    def query_tile(qi, carry):
        def body(jj, c):
            scores(qi, 2 * jj + 1, 1)
            update(2 * jj, everything, from_scratch(0, everything), False)
            scores(qi, 2 * jj + 2, 0)
            update(2 * jj + 1, everything, from_scratch(1, everything), False)
            return c

        lax.fori_loop(0, qi, body, 0)
        update(2 * qi, top, from_scratch(0, top), True)
        update(2 * qi, bottom, from_scratch(0, bottom), False)
        ks = k_ref[key_rows(2 * qi + 1), :]
        last = [lax.dot_general(q_ref[query_rows(qi, bottom), head_cols(half)], ks[:, head_cols(half)], nt,
                                preferred_element_type=F32) for half in range(2)]
        update(2 * qi + 1, bottom, last, True)
        out = [acc_scr[half] / acc_scr[half][:, V_DIM:V_DIM + 1] for half in range(2)]
        low = lax.broadcasted_iota(jnp.int32, (tq, LANES), 1) < V_DIM
        o_ref[query_rows(qi, everything), :] = jnp.where(low, out[0], pltpu.roll(out[1], V_DIM, 1)).astype(o_ref.dtype)
        reset()
        scores(jnp.minimum(qi + 1, nq - 1), 0, 0)
        return carry

    lax.fori_loop(0, nq, query_tile, 0)


def _flash_attention(q, k, v, *, tk):
    b, pairs, s, _ = q.shape
    tq = 2 * tk
    resident = pl.Buffered(1)
    return pl.pallas_call(
        functools.partial(_flash_kernel, tq=tq, tk=tk),
        grid=(b, pairs),
        in_specs=[pl.BlockSpec((None, None, s, 2 * LANES), lambda bi, p: (bi, p, 0, 0), pipeline_mode=resident),
                  pl.BlockSpec((None, None, s, 2 * LANES), lambda bi, p: (bi, p, 0, 0), pipeline_mode=resident),
                  pl.BlockSpec((None, None, s, 2 * LANES), lambda bi, p: (bi, p, 0, 0), pipeline_mode=resident)],
        out_specs=pl.BlockSpec((None, s, LANES), lambda bi, p: (bi, 0, p)),
        out_shape=jax.ShapeDtypeStruct((b, s, pairs * LANES), BF16),
        scratch_shapes=[pltpu.VMEM((2, 2, tq, tk), F32),
                        pltpu.VMEM((2, tq, LANES), F32), pltpu.VMEM((2, tq, LANES), F32)],
        compiler_params=_params(("arbitrary", "arbitrary")),
        name="flash_attention",
    )(q, k, v)


def _route_kernel(x_ref, g_ref, wr_ref, meta_ref, h_ref, start_ref, cnt_ref, carry):
    i = pl.program_id(0)
    t = x_ref.shape[0]

    @pl.when(i == 0)
    def _():
        carry[...] = jnp.zeros(carry.shape, F32)

    h = _rms(x_ref[...], g_ref[...])
    h_ref[...] = h
    h_hi = h.astype(BF16)
    h_lo = (h - h_hi.astype(F32)).astype(BF16)
    logits = (jnp.dot(h_hi, wr_ref[0], preferred_element_type=F32)
              + jnp.dot(h_hi, wr_ref[1], preferred_element_type=F32)
              + jnp.dot(h_lo, wr_ref[0], preferred_element_type=F32))
    lane = lax.broadcasted_iota(jnp.int32, (t, LANES), 1)
    lg = jnp.where(lane < N_EXPERTS, logits, -jnp.inf)
    m1 = jnp.max(lg, axis=-1, keepdims=True)
    i1 = jnp.min(jnp.where(lg == m1, lane, LANES), axis=-1, keepdims=True)
    lg2 = jnp.where(lane == i1, -jnp.inf, lg)
    m2 = jnp.max(lg2, axis=-1, keepdims=True)
    i2 = jnp.min(jnp.where(lg2 == m2, lane, LANES), axis=-1, keepdims=True)
    e2 = jnp.exp(m2 - m1)
    w1 = 1.0 / (1.0 + e2)
    w2 = e2 / (1.0 + e2)
    hot1 = lane == i1
    hot2 = lane == i2
    onehot = (hot1 | hot2).astype(BF16)
    r = lax.broadcasted_iota(jnp.int32, (t, t), 0)
    c = lax.broadcasted_iota(jnp.int32, (t, t), 1)
    lower = (c < r).astype(BF16)
    before = jnp.dot(lower, onehot, preferred_element_type=F32) + carry[...]
    start_ref[...] = carry[...]
    rank1 = jnp.sum(jnp.where(hot1, before, 0.0), axis=-1, keepdims=True)
    rank2 = jnp.sum(jnp.where(hot2, before, 0.0), axis=-1, keepdims=True)
    carry[...] += jnp.sum(onehot.astype(F32), axis=0, keepdims=True)
    meta = jnp.zeros((t, LANES), F32)
    for k, val in enumerate((i1.astype(F32), i2.astype(F32), rank1, rank2, w1, w2)):
        meta = jnp.where(lane == k, val, meta)
    meta_ref[...] = meta
    cnt_ref[...] = carry[...]


def _route(x, g, w_router, *, tm):
    m, d = x.shape
    wr = jnp.zeros((d, LANES), F32).at[:, :N_EXPERTS].set(w_router)
    wr_hi = wr.astype(BF16)
    wr = jnp.stack([wr_hi, (wr - wr_hi.astype(F32)).astype(BF16)])
    return pl.pallas_call(
        _route_kernel,
        grid=(m // tm,),
        in_specs=[pl.BlockSpec((tm, d), lambda i: (i, 0)),
                  pl.BlockSpec((1, d), lambda i: (0, 0)),
                  pl.BlockSpec((2, d, LANES), lambda i: (0, 0, 0))],
        out_specs=[pl.BlockSpec((tm, LANES), lambda i: (i, 0)),
                   pl.BlockSpec((tm, d), lambda i: (i, 0)),
                   pl.BlockSpec((None, 1, LANES), lambda i: (i, 0, 0)),
                   pl.BlockSpec((1, LANES), lambda i: (0, 0))],
        out_shape=[jax.ShapeDtypeStruct((m, LANES), F32),
                   jax.ShapeDtypeStruct((m, d), F32),
                   jax.ShapeDtypeStruct((m // tm, 1, LANES), F32),
                   jax.ShapeDtypeStruct((1, LANES), F32)],
        scratch_shapes=[pltpu.VMEM((1, LANES), F32)],
        compiler_params=_params(("arbitrary",)),
        name="route",
    )(x, g.reshape(1, d), wr)


def _invert_kernel(start_ref, base_ref, meta_ref, init_ref, list_ref, obuf, sem):
    del init_ref
    s = pl.program_id(0)
    t = meta_ref.shape[0]
    meta = meta_ref[...]
    second_expert = meta[:, 1:2]
    by_lane = meta.T
    i1, i2, r1, r2 = by_lane[0:1], by_lane[1:2], by_lane[2:3], by_lane[3:4]
    token = lax.broadcasted_iota(jnp.int32, (t, 1), 0) + s * t
    lane = lax.broadcasted_iota(jnp.int32, (t, LANES), 1)
    place = lax.broadcasted_iota(jnp.int32, (t, t), 0).astype(F32)
    copies = []
    for e in range(N_EXPERTS):
        first, second = i1 == e, i2 == e
        start = start_ref[s * N_EXPERTS + e]
        local = jnp.where(first, r1, r2) - start.astype(F32)
        onehot = ((local == place) & (first | second)).astype(BF16)
        code = 2 * token + (second_expert == e).astype(jnp.int32) + 1
        pieces = jnp.where(lane == 0, (code >> 8).astype(F32),
                           jnp.where(lane == 1, (code & 255).astype(F32), 0.0)).astype(BF16)
        obuf[e] = jnp.dot(onehot, pieces, preferred_element_type=F32)
        copy = pltpu.make_async_copy(obuf.at[e], list_ref.at[pl.ds(base_ref[e] + start, t)], sem)
        copy.start()
        copies.append(copy)
    for copy in copies:
        copy.wait()


def _invert(meta, starts, base, *, tm, rows):
    m = meta.shape[0]
    grid_spec = pltpu.PrefetchScalarGridSpec(
        num_scalar_prefetch=2,
        grid=(m // tm,),
        in_specs=[pl.BlockSpec((tm, LANES), lambda i, st, ba: (i, 0)),
                  pl.BlockSpec(memory_space=pl.ANY)],
        out_specs=pl.BlockSpec(memory_space=pl.ANY),
        scratch_shapes=[pltpu.VMEM((N_EXPERTS, tm, LANES), F32), pltpu.SemaphoreType.DMA(())],
    )
    return pl.pallas_call(
        _invert_kernel,
        grid_spec=grid_spec,
        out_shape=jax.ShapeDtypeStruct((rows, LANES), F32),
        input_output_aliases={3: 0},
        compiler_params=_params(("arbitrary",)),
        name="invert",
    )(starts, base, meta, jnp.zeros((rows, LANES), F32))


def _grouped_kernel(te_ref, tv_ref, tb_ref, tp_ref, cur_ref, nxt_ref, dprev_ref, x_hbm, wg_ref, wu_ref, wd_ref,
                    y_hbm, xbuf, hbuf, ybuf, gsem, ssem):
    del te_ref, tb_ref, tp_ref
    i = pl.program_id(0)
    last = pl.num_programs(0) - 1
    tm = xbuf.shape[1]
    slot = i % 2
    other = 1 - slot

    def gather_row(ids_ref, j, buf):
        return pltpu.make_async_copy(x_hbm.at[pl.ds(ids_ref[j], 1)], xbuf.at[buf, pl.ds(j, 1)], gsem.at[buf])

    def scatter_row(j, buf):
        return pltpu.make_async_copy(ybuf.at[buf, pl.ds(j, 1)], y_hbm.at[pl.ds(dprev_ref[j], 1)], ssem.at[buf])

    def gather_looped(ids_ref, buf):
        def body(j, carry):
            gather_row(ids_ref, j, buf).start()
            return carry
        lax.fori_loop(0, tm, body, 0, unroll=8)

    def scatter_looped(buf):
        def body(j, carry):
            scatter_row(j, buf).start()
            return carry
        lax.fori_loop(0, tm, body, 0, unroll=8)

    @pl.when(i == 0)
    def _():
        ybuf[...] = jnp.zeros(ybuf.shape, F32)
        gather_looped(cur_ref, 0)

    pltpu.make_async_copy(x_hbm.at[pl.ds(0, tm)], xbuf.at[slot], gsem.at[slot]).wait()

    @pl.when(i >= 1)
    def _():
        pltpu.make_async_copy(ybuf.at[slot], y_hbm.at[pl.ds(0, tm)], ssem.at[slot]).wait()

    @pl.when(tv_ref[i] > 0)
    def _():
        hbuf[...] = xbuf[slot].astype(BF16)
        for j in range(tm):
            gather_row(nxt_ref, j, other).start(priority=0)
            scatter_row(j, other).start(priority=1)
        h = hbuf[...]
        gate = jnp.dot(h, wg_ref[...], preferred_element_type=F32)
        up = jnp.dot(h, wu_ref[...], preferred_element_type=F32)
        act = (gate * jax.nn.sigmoid(gate) * up).astype(BF16)
        ybuf[slot] = jnp.dot(act, wd_ref[...], preferred_element_type=F32)

    @pl.when(tv_ref[i] == 0)
    def _():
        gather_looped(nxt_ref, other)
        scatter_looped(other)

    @pl.when(i == last)
    def _():
        pltpu.make_async_copy(x_hbm.at[pl.ds(0, tm)], xbuf.at[other], gsem.at[other]).wait()
        pltpu.make_async_copy(ybuf.at[other], y_hbm.at[pl.ds(0, tm)], ssem.at[other]).wait()


def _grouped_swiglu(x, src, dst, tile_expert, tile_valid, tile_block, prev_block, wg, wu, wd, *, tm, y_rows):
    m, d = x.shape
    ff = wg.shape[2]
    n_tiles = tile_expert.shape[0]
    resident = pl.Buffered(1)
    smem_block = functools.partial(pl.BlockSpec, (tm,), memory_space=pltpu.SMEM)
    grid_spec = pltpu.PrefetchScalarGridSpec(
        num_scalar_prefetch=4,
        grid=(n_tiles,),
        in_specs=[smem_block(lambda i, te, tv, tb, tp: (tb[i],)),
                  smem_block(lambda i, te, tv, tb, tp: (tb[jnp.minimum(i + 1, n_tiles - 1)],)),
                  smem_block(lambda i, te, tv, tb, tp: (tp[i],)),
                  pl.BlockSpec(memory_space=pl.ANY),
                  pl.BlockSpec((None, d, ff), lambda i, te, tv, tb, tp: (te[i], 0, 0), pipeline_mode=resident),
                  pl.BlockSpec((None, d, ff), lambda i, te, tv, tb, tp: (te[i], 0, 0), pipeline_mode=resident),
                  pl.BlockSpec((None, ff, d), lambda i, te, tv, tb, tp: (te[i], 0, 0), pipeline_mode=resident)],
        out_specs=pl.BlockSpec(memory_space=pl.ANY),
        scratch_shapes=[pltpu.VMEM((2, tm, d), F32), pltpu.VMEM((tm, d), BF16), pltpu.VMEM((2, tm, d), F32),
                        pltpu.SemaphoreType.DMA((2,)), pltpu.SemaphoreType.DMA((2,))],
    )
    return pl.pallas_call(
        _grouped_kernel,
        grid_spec=grid_spec,
        out_shape=jax.ShapeDtypeStruct((y_rows, d), F32),
        compiler_params=_params(("arbitrary",), vmem=GROUPED_VMEM_LIMIT),
        name="grouped_swiglu",
    )(tile_expert, tile_valid, tile_block, prev_block, src, src, dst, x, wg, wu, wd)


def _combine_kernel(x_ref, meta_ref, g_ref, y0_ref, y1_ref, o_ref):
    meta = meta_ref[...]
    x = x_ref[...] + (meta[:, 4:5] * y0_ref[...] + meta[:, 5:6] * y1_ref[...])
    o_ref[...] = _rms(x, g_ref[...])


def _combine(x, meta, y, g, *, tm):
    m, d = x.shape
    return pl.pallas_call(
        _combine_kernel,
        grid=(m // tm,),
        in_specs=[pl.BlockSpec((tm, d), lambda i: (i, 0)),
                  pl.BlockSpec((tm, LANES), lambda i: (i, 0)),
                  pl.BlockSpec((1, d), lambda i: (0, 0)),
                  pl.BlockSpec((tm, d), lambda i: (i, 0)),
                  pl.BlockSpec((tm, d), lambda i: (m // tm + i, 0))],
        out_specs=pl.BlockSpec((tm, d), lambda i: (i, 0)),
        out_shape=jax.ShapeDtypeStruct((m, d), F32),
        compiler_params=_params(("arbitrary",)),
        name="combine",
    )(x, meta, g.reshape(1, d), y, y)


def _moe_layer(x, g, w_router, wg, wu, wd, final_g, *, tm):
    m, d = x.shape
    route_tm = 512
    meta, hn, starts, counts = _route(x, g, w_router, tm=route_tm)
    cnt = counts[0, :N_EXPERTS].astype(jnp.int32)
    tiles = (cnt + tm - 1) // tm
    tile_end = jnp.cumsum(tiles)
    first_tile = tile_end - tiles
    n_tiles = (2 * m) // tm + N_EXPERTS
    tile_id = jnp.arange(n_tiles, dtype=jnp.int32)
    tile_expert = jnp.sum(tile_id[:, None] >= tile_end[None, :], axis=1).astype(jnp.int32)
    tile_valid = (tile_expert < N_EXPERTS).astype(jnp.int32)
    tile_expert = jnp.minimum(tile_expert, N_EXPERTS - 1)
    assert route_tm <= tm
    base = (first_tile + jnp.arange(N_EXPERTS, dtype=jnp.int32)) * tm
    list_blocks = n_tiles + N_EXPERTS
    lists = _invert(meta, starts[:, 0, :N_EXPERTS].astype(jnp.int32).reshape(-1), base.astype(jnp.int32),
                    tm=route_tm, rows=list_blocks * tm)
    code = (lists[:, 0] * 256.0 + lists[:, 1]).astype(jnp.int32) - 1
    row_in_tile = jnp.tile(jnp.arange(tm, dtype=jnp.int32), list_blocks)
    src = jnp.where(code < 0, row_in_tile, code >> 1)
    dst = jnp.where(code < 0, 2 * m + row_in_tile, (code & 1) * m + (code >> 1))
    empty_block = list_blocks - 1
    tile_block = tile_id + tile_expert
    dst_block = jnp.where(tile_valid > 0, tile_block, empty_block).astype(jnp.int32)
    prev_block = jnp.concatenate([jnp.full((1,), empty_block, jnp.int32), dst_block[:-1]])
    tile_block = jnp.where(tile_valid > 0, tile_block, empty_block).astype(jnp.int32)
    y = _grouped_swiglu(hn, src, dst, tile_expert, tile_valid, tile_block, prev_block, wg, wu, wd, tm=tm,
                        y_rows=2 * m + tm)
    return _combine(x, meta, y, final_g, tm=512)


def _mla_weights(w_q_up, w_kv_up):
    hd = QK_NOPE + QK_ROPE
    wq = w_q_up.reshape(Q_LORA, MLA_HEADS, hd)
    wq_pad = jnp.pad(wq, ((0, 0), (0, 0), (0, LANES - hd))).reshape(Q_LORA, MLA_HEADS * LANES)
    wkv = w_kv_up.reshape(KV_LORA, MLA_HEADS, QK_NOPE + V_DIM)
    wk_pad = jnp.pad(wkv[:, :, :QK_NOPE], ((0, 0), (0, 0), (0, LANES - QK_NOPE)))
    wk_pad = wk_pad.reshape(KV_LORA, MLA_HEADS * LANES)
    wv = jnp.pad(wkv[:, :, QK_NOPE:], ((0, 0), (0, 0), (0, LANES - V_DIM))).reshape(KV_LORA, MLA_HEADS * LANES)
    place = jnp.zeros((QK_ROPE, LANES), F32).at[jnp.arange(QK_ROPE), QK_NOPE + jnp.arange(QK_ROPE)].set(1.0)
    return wq_pad.astype(BF16), wk_pad.astype(BF16), place.astype(BF16), wv.astype(BF16)


def _rope_lane_tables(positions):
    half = QK_ROPE // 2
    inv = ROPE_THETA ** (-jnp.arange(0, QK_ROPE, 2, dtype=F32) / QK_ROPE)
    ang = positions.astype(F32)[..., None] * inv
    cos, sin = jnp.cos(ang), jnp.sin(ang)
    shape = positions.shape
    ones = jnp.ones(shape + (QK_NOPE,), F32)
    zeros_n = jnp.zeros(shape + (QK_NOPE,), F32)
    zeros_h = jnp.zeros(shape + (half,), F32)
    pad = jnp.zeros(shape + (LANES - QK_NOPE - QK_ROPE,), F32)
    c = jnp.concatenate([ones, cos, cos, pad], axis=-1)
    sa = jnp.concatenate([zeros_n, -sin, zeros_h, pad], axis=-1)
    sb = jnp.concatenate([zeros_n, zeros_h, sin, pad], axis=-1)
    return c, sa, sb


def kernel(x, positions, mix_norm_g, ffn_norm_g, a_w_qkv, a_w_out, b_w_in, b_q_norm_g, b_kv_norm_g,
           b_w_q_up, b_w_kv_up, b_w_out, ffn_w_gate, ffn_w_up, ffn_w_down, moe_w_router, moe_w_gate,
           moe_w_up, moe_w_down, final_norm_g):
    b, s, d = x.shape
    m = b * s
    xf = x.reshape(m, d)

    group_cols = 3 * DSA_WIDTH
    q_scale = jnp.where(jnp.arange(a_w_qkv.shape[2]) % group_cols < DSA_WIDTH, DSA_HEAD_DIM ** -0.5, 1.0)
    w_qkv = (a_w_qkv[0] * q_scale).astype(BF16)
    state = None
    for group, (_, dil) in enumerate(DIL_PAIRS):
        w_group = w_qkv[:, group * group_cols:(group + 1) * group_cols]
        qkv_v = _qkv_project(xf, mix_norm_g[0], w_group, dil=dil, tm=1024 if dil == 1 else 512)
        qkv_v = qkv_v.reshape(b, s // dil, dil * group_cols)
        state = _merge_group(qkv_v, positions, state, group=group, qb=4)
    x1 = _unpermute_project(xf, state, a_w_out[0].astype(BF16), dil=DIL_PAIRS[-1][1], tm=512)
    x2 = _swiglu(x1, ffn_norm_g[0], ffn_w_gate[0].astype(BF16), ffn_w_up[0].astype(BF16),
                 ffn_w_down[0].astype(BF16), tm=512)

    wq_pad, wk_pad, wkr, wv = _mla_weights(b_w_q_up[0], b_w_kv_up[0])
    q, k, v = _mla_project(x2.reshape(b, s, d), mix_norm_g[1], b_w_in[0].astype(BF16), b_q_norm_g[0],
                           b_kv_norm_g[0], wq_pad, wk_pad, wkr, wv, _rope_lane_tables(positions), tm=512)
    attn = _flash_attention(q, k, v, tk=512)
    x3 = _matmul_residual(x2, attn.reshape(m, MLA_HEADS * V_DIM), b_w_out[0].astype(BF16), tm=1024)
    out = _moe_layer(x3, ffn_norm_g[1], moe_w_router[0], moe_w_gate[0].astype(BF16),
                     moe_w_up[0].astype(BF16), moe_w_down[0].astype(BF16), final_norm_g,
                     tm=512)
    return out.reshape(b, s, d)
```

```python
import functools

import jax
import jax.numpy as jnp
from jax import lax
from jax.experimental import pallas as pl
from jax.experimental.pallas import tpu as pltpu

F32 = jnp.float32
BF16 = jnp.bfloat16

RMS_EPS = 1e-6
BLOCK = 128
LANES = 128
DIL_PAIRS = ((128, 1), (512, 4), (2048, 16))
DSA_HEADS = 8
DSA_HEAD_DIM = 64
DSA_WIDTH = DSA_HEADS * DSA_HEAD_DIM
MLA_HEADS = 16
QK_NOPE = 64
QK_ROPE = 32
V_DIM = 64
Q_LORA = 384
KV_LORA = 256
ROPE_THETA = 10000.0
N_EXPERTS = 8
MASK_VALUE = -1e30
LOG2_E = 1.4426950408889634
VMEM_LIMIT = 56 << 20
GROUPED_VMEM_LIMIT = 60 << 20


def _params(semantics, vmem=VMEM_LIMIT):
    return pltpu.CompilerParams(dimension_semantics=semantics, vmem_limit_bytes=vmem)


def _rms(x, g):
    ms = jnp.mean(x * x, axis=-1, keepdims=True)
    return x * lax.rsqrt(ms + RMS_EPS) * g


def _residue_slot(dil):
    if dil == 16:
        return tuple(4 * (r % 4) + r // 4 for r in range(dil))
    return tuple(range(dil))


def _deinterleave(tm, dil):
    n = tm // dil
    out_row = jnp.arange(tm, dtype=jnp.int32)
    source = (out_row % n) * dil + out_row // n
    return (source[:, None] == out_row[None, :]).astype(BF16)


def _qkv_kernel(x_ref, g_ref, perm_ref, w_ref, o_ref, *, dil):
    tm = x_ref.shape[0]
    n = tm // dil
    width = w_ref.shape[1]
    h = _rms(x_ref[...], g_ref[...]).astype(BF16)
    if dil > 1:
        h = jnp.dot(perm_ref[...], h, preferred_element_type=F32).astype(BF16)
    res = jnp.dot(h, w_ref[...], preferred_element_type=F32)
    for r, slot in enumerate(_residue_slot(dil)):
        o_ref[:, slot * width:(slot + 1) * width] = res[r * n:(r + 1) * n, :].astype(o_ref.dtype)


def _qkv_project(x, g, w, *, dil, tm):
    m, d = x.shape
    width = w.shape[1]
    n = tm // dil
    perm = _deinterleave(tm, dil)
    return pl.pallas_call(
        functools.partial(_qkv_kernel, dil=dil),
        grid=(m // tm,),
        in_specs=[pl.BlockSpec((tm, d), lambda i: (i, 0)),
                  pl.BlockSpec((1, d), lambda i: (0, 0)),
                  pl.BlockSpec((tm, tm), lambda i: (0, 0)),
                  pl.BlockSpec((d, width), lambda i: (0, 0))],
        out_specs=pl.BlockSpec((n, dil * width), lambda i: (i, 0)),
        out_shape=jax.ShapeDtypeStruct((m // dil, dil * width), BF16),
        compiler_params=_params(("arbitrary",)),
        name=f"qkv_project_d{dil}",
    )(x, g.reshape(1, d), perm, w)


def _matmul_residual_kernel(x_ref, a_ref, w_ref, o_ref):
    o_ref[...] = x_ref[...] + jnp.dot(a_ref[...], w_ref[...], preferred_element_type=F32)


def _matmul_residual(x, a, w, *, tm):
    m, d = x.shape
    k = a.shape[1]
    return pl.pallas_call(
        _matmul_residual_kernel,
        grid=(m // tm,),
        in_specs=[pl.BlockSpec((tm, d), lambda i: (i, 0)),
                  pl.BlockSpec((tm, k), lambda i: (i, 0)),
                  pl.BlockSpec((k, d), lambda i: (0, 0))],
        out_specs=pl.BlockSpec((tm, d), lambda i: (i, 0)),
        out_shape=jax.ShapeDtypeStruct((m, d), F32),
        compiler_params=_params(("arbitrary",)),
        name="matmul_residual",
    )(x, a, w)


def _merge_kernel(*refs, qb, span, has_prev, emit_lse):
    q_ref, k_ref, v_ref, prow_ref = refs[:4]
    pos = 4
    if has_prev:
        oprev_ref, lprev_ref = refs[pos:pos + 2]
        pos += 2
    if emit_lse:
        perm_ref, o_ref, lse_ref = refs[pos:pos + 3]
        pos += 3
    else:
        o_ref = refs[pos]
        pos += 1
    kbuf, vbuf, pbuf = refs[pos:pos + 3]
    if emit_lse:
        o_scr, lse_scr = refs[pos + 3:pos + 5]

    n = pl.program_id(2)
    tq = qb * BLOCK

    @pl.when(n == 0)
    def _():
        kbuf[0:BLOCK, :] = jnp.zeros((BLOCK, DSA_WIDTH), BF16)
        vbuf[0:BLOCK, :] = jnp.zeros((BLOCK, DSA_WIDTH), BF16)
        pbuf[:, 0:BLOCK] = jnp.zeros((1, BLOCK), jnp.int32)

    kbuf[BLOCK:, :] = k_ref[...]
    vbuf[BLOCK:, :] = v_ref[...]
    pbuf[:, BLOCK:] = prow_ref[...]

    lane = lax.broadcasted_iota(jnp.int32, (BLOCK, LANES), 1)
    sub = lax.broadcasted_iota(jnp.int32, (BLOCK, LANES), 0)
    low = lane < DSA_HEAD_DIM
    diag = lane == sub
    row = lax.broadcasted_iota(jnp.int32, (BLOCK, 2 * BLOCK), 0)
    col = lax.broadcasted_iota(jnp.int32, (BLOCK, 2 * BLOCK), 1)
    rel = row + BLOCK - col
    band = (rel >= 0) & (rel <= span)
    first_valid = band & ((col >= BLOCK) | (n > 0))
    spread = (lax.broadcasted_iota(jnp.int32, (LANES, DSA_WIDTH), 0)
              == lax.broadcasted_iota(jnp.int32, (LANES, DSA_WIDTH), 1) // DSA_HEAD_DIM).astype(BF16)

    def per_head(c):
        hi = c.astype(BF16)
        lo = (c - hi.astype(F32)).astype(BF16)
        return (jnp.dot(hi, spread, preferred_element_type=F32)
                + jnp.dot(lo, spread, preferred_element_type=F32))

    def lane_to_column(v):
        return jnp.sum(jnp.where(diag, v, 0.0), axis=1, keepdims=True)

    for i in range(qb):
        rows = slice(i * BLOCK, (i + 1) * BLOCK)
        keys = slice(i * BLOCK, (i + 2) * BLOCK)
        kk = kbuf[keys, :]
        vv = vbuf[keys, :]
        pk = pbuf[:, keys]
        pq_row = pbuf[:, (i + 1) * BLOCK:(i + 2) * BLOCK]
        pq = ((lane_to_column((pq_row >> 12).astype(F32)).astype(jnp.int32) << 12)
              | lane_to_column((pq_row & 4095).astype(F32)).astype(jnp.int32))
        dist = jnp.abs(pq - pk).astype(F32)
        neg_dist = jnp.where(first_valid if i == 0 else band, -dist, MASK_VALUE)
        q = q_ref[rows, :]
        m_tile = jnp.zeros((BLOCK, LANES), F32)
        l_tile = jnp.ones((BLOCK, LANES), F32)
        pv_pairs = []
        for p in range(DSA_HEADS // 2):
            cols = slice(p * LANES, (p + 1) * LANES)
            q2, k2, v2 = q[:, cols], kk[:, cols], vv[:, cols]
            pvs = []
            for half in range(2):
                h = 2 * p + half
                slope = 2.0 ** (-8.0 * (h + 1) / DSA_HEADS)
                qm = jnp.where(low if half == 0 else ~low, q2, jnp.zeros_like(q2))
                s = lax.dot_general(qm, k2, (((1,), (1,)), ((), ())), preferred_element_type=F32)
                s = s + slope * neg_dist
                m = jnp.max(s, axis=-1, keepdims=True)
                e = jnp.exp(s - m)
                m_tile = jnp.where(lane == h, m, m_tile)
                l_tile = jnp.where(lane == h, jnp.sum(e, axis=-1, keepdims=True), l_tile)
                pvs.append(jnp.dot(e.astype(BF16), v2, preferred_element_type=F32))
            pv_pairs.append(jnp.where(low, pvs[0], pvs[1]))
        pv_all = jnp.concatenate(pv_pairs, axis=1)
        lse = m_tile + jnp.log(l_tile)
        if has_prev:
            lp = lprev_ref[rows, :]
            mx = jnp.maximum(lp, lse)
            wa = jnp.exp(lp - mx)
            wb = jnp.exp(lse - mx)
            den = wa + wb
            c_new = wb / (den * l_tile)
            lse = mx + jnp.log(den)
            out = pv_all * per_head(c_new) + oprev_ref[rows, :].astype(F32) * per_head(wa / den)
        else:
            out = pv_all * per_head(1.0 / l_tile)
        if emit_lse:
            o_scr[rows, :] = out.astype(BF16)
            lse_scr[rows, :] = lse
        else:
            o_ref[rows, :] = out.astype(o_ref.dtype)

    if emit_lse:
        part = tq // 4
        o_perm = jnp.dot(perm_ref[...], o_scr[...], preferred_element_type=F32)
        for r in range(4):
            o_ref[:, r * DSA_WIDTH:(r + 1) * DSA_WIDTH] = o_perm[r * part:(r + 1) * part, :].astype(o_ref.dtype)
            lse_ref[:, r * LANES:(r + 1) * LANES] = lse_scr[pl.ds(r, part, stride=4), :]

    tail = slice(qb * BLOCK, (qb + 1) * BLOCK)
    kbuf[0:BLOCK, :] = kbuf[tail, :]
    vbuf[0:BLOCK, :] = vbuf[tail, :]
    pbuf[:, 0:BLOCK] = pbuf[:, tail]


def _merge_group(qkv_v, positions, prev, *, group, qb):
    b, s = positions.shape
    window, dil = DIL_PAIRS[group]
    last = group == len(DIL_PAIRS) - 1
    span = window // dil
    sub = s // dil
    tq = qb * BLOCK
    nb = sub // tq
    residue_of_slot = sorted(range(dil), key=_residue_slot(dil).__getitem__)
    prow = positions.reshape(b, sub, dil).transpose(0, 2, 1)[:, jnp.asarray(residue_of_slot), :]
    prow = prow.reshape(b, dil, 1, sub)

    def qkv_spec(which):
        return pl.BlockSpec((None, tq, DSA_WIDTH), lambda bi, r, n: (bi, n, r * 3 + which))

    in_specs = [qkv_spec(0), qkv_spec(1), qkv_spec(2),
                pl.BlockSpec((None, None, 1, tq), lambda bi, r, n: (bi, r, 0, n))]
    args = [qkv_v, qkv_v, qkv_v, prow]
    if prev is not None:
        in_specs += [pl.BlockSpec((None, tq, DSA_WIDTH), lambda bi, r, n: (bi, n, r)),
                     pl.BlockSpec((None, tq, LANES), lambda bi, r, n: (bi, n, r))]
        args += list(prev)
    scratch = [pltpu.VMEM((tq + BLOCK, DSA_WIDTH), BF16), pltpu.VMEM((tq + BLOCK, DSA_WIDTH), BF16),
               pltpu.VMEM((1, tq + BLOCK), jnp.int32)]
    if last:
        out_specs = [pl.BlockSpec((None, tq, DSA_WIDTH), lambda bi, r, n: (bi, n, r))]
        out_shape = [jax.ShapeDtypeStruct((b, sub, dil * DSA_WIDTH), BF16)]
    else:
        in_specs.append(pl.BlockSpec((tq, tq), lambda bi, r, n: (0, 0)))
        args.append(_deinterleave(tq, 4))
        out_specs = [pl.BlockSpec((None, tq // 4, 4 * DSA_WIDTH), lambda bi, r, n: (bi, n, r)),
                     pl.BlockSpec((None, tq // 4, 4 * LANES), lambda bi, r, n: (bi, n, r))]
        out_shape = [jax.ShapeDtypeStruct((b, sub // 4, 4 * dil * DSA_WIDTH), BF16),
                     jax.ShapeDtypeStruct((b, sub // 4, 4 * dil * LANES), F32)]
        scratch += [pltpu.VMEM((tq, DSA_WIDTH), BF16), pltpu.VMEM((tq, LANES), F32)]
    outs = pl.pallas_call(
        functools.partial(_merge_kernel, qb=qb, span=span, has_prev=prev is not None, emit_lse=not last),
        grid=(b, dil, nb),
        in_specs=in_specs,
        out_specs=out_specs,
        out_shape=out_shape,
        scratch_shapes=scratch,
        compiler_params=_params(("arbitrary", "arbitrary", "arbitrary")),
        name=f"dilated_group{group}",
    )(*args)
    return outs[0] if last else tuple(outs)


def _unpermute_project_kernel(x_ref, a_ref, perm_ref, w_ref, o_ref, *, dil):
    width = w_ref.shape[0]
    stacked = jnp.concatenate([a_ref[:, slot * width:(slot + 1) * width] for slot in _residue_slot(dil)],
                              axis=0)
    tokens = jnp.dot(perm_ref[...], stacked, preferred_element_type=F32).astype(BF16)
    o_ref[...] = x_ref[...] + jnp.dot(tokens, w_ref[...], preferred_element_type=F32)


def _unpermute_project(x, a, w, *, dil, tm):
    m, d = x.shape
    width = w.shape[0]
    return pl.pallas_call(
        functools.partial(_unpermute_project_kernel, dil=dil),
        grid=(m // tm,),
        in_specs=[pl.BlockSpec((tm, d), lambda i: (i, 0)),
                  pl.BlockSpec((tm // dil, dil * width), lambda i: (i, 0)),
                  pl.BlockSpec((tm, tm), lambda i: (0, 0)),
                  pl.BlockSpec((width, d), lambda i: (0, 0))],
        out_specs=pl.BlockSpec((tm, d), lambda i: (i, 0)),
        out_shape=jax.ShapeDtypeStruct((m, d), F32),
        compiler_params=_params(("arbitrary",)),
        name="unpermute_project",
    )(x, a.reshape(m // dil, dil * width), _deinterleave(tm, dil).T, w)


def _swiglu_kernel(x_ref, g_ref, wg_ref, wu_ref, wd_ref, o_ref):
    x = x_ref[...]
    h = _rms(x, g_ref[...]).astype(BF16)
    gate = jnp.dot(h, wg_ref[...], preferred_element_type=F32)
    up = jnp.dot(h, wu_ref[...], preferred_element_type=F32)
    act = (gate * jax.nn.sigmoid(gate) * up).astype(BF16)
    o_ref[...] = x + jnp.dot(act, wd_ref[...], preferred_element_type=F32)


def _swiglu(x, g, wg, wu, wd, *, tm):
    m, d = x.shape
    ff = wg.shape[1]
    resident = pl.Buffered(1)
    return pl.pallas_call(
        _swiglu_kernel,
        grid=(m // tm,),
        in_specs=[pl.BlockSpec((tm, d), lambda i: (i, 0)),
                  pl.BlockSpec((1, d), lambda i: (0, 0)),
                  pl.BlockSpec((d, ff), lambda i: (0, 0), pipeline_mode=resident),
                  pl.BlockSpec((d, ff), lambda i: (0, 0), pipeline_mode=resident),
                  pl.BlockSpec((ff, d), lambda i: (0, 0), pipeline_mode=resident)],
        out_specs=pl.BlockSpec((tm, d), lambda i: (i, 0)),
        out_shape=jax.ShapeDtypeStruct((m, d), F32),
        compiler_params=_params(("arbitrary",)),
        name="swiglu",
    )(x, g.reshape(1, d), wg, wu, wd)


def _rope_lanes(x, c, sa, sb):
    return x * c + pltpu.roll(x, LANES - QK_ROPE // 2, 1) * sa + pltpu.roll(x, QK_ROPE // 2, 1) * sb


def _mla_project_kernel(x_ref, g_ref, win_ref, gq_ref, gkv_ref, wq_ref, wk_ref, wkr_ref, wv_ref,
                        c_ref, sa_ref, sb_ref, q_ref, k_ref, v_ref, *, scale):
    h = _rms(x_ref[...], g_ref[...]).astype(BF16)
    a = jnp.dot(h, win_ref[...], preferred_element_type=F32)
    c_q = _rms(a[:, :Q_LORA], gq_ref[...]).astype(BF16)
    c_kv = _rms(a[:, Q_LORA:Q_LORA + KV_LORA], gkv_ref[...]).astype(BF16)
    k_rope = a[:, Q_LORA + KV_LORA:].astype(BF16)
    q = jnp.dot(c_q, wq_ref[...], preferred_element_type=F32)
    k = jnp.dot(c_kv, wk_ref[...], preferred_element_type=F32)
    v = jnp.dot(c_kv, wv_ref[...], preferred_element_type=F32)
    c, sa, sb = c_ref[...], sa_ref[...], sb_ref[...]
    ones_lane = lax.broadcasted_iota(jnp.int32, (1, 2 * LANES), 1) % LANES == V_DIM
    shared = _rope_lanes(jnp.dot(k_rope, wkr_ref[...], preferred_element_type=F32), c, sa, sb)
    for p in range(MLA_HEADS // 2):
        qs, ks = [], []
        for half in range(2):
            cols = slice((2 * p + half) * LANES, (2 * p + half + 1) * LANES)
            qs.append((_rope_lanes(q[:, cols], c, sa, sb) * scale).astype(BF16))
            ks.append((k[:, cols] + shared).astype(BF16))
        q_ref[p] = jnp.concatenate(qs, axis=1)
        k_ref[p] = jnp.concatenate(ks, axis=1)
        v_ref[p] = jnp.where(ones_lane, 1.0, v[:, 2 * p * LANES:2 * (p + 1) * LANES]).astype(BF16)


def _mla_project(x, g, w_in, gq, gkv, wq_pad, wk_pad, wkr, wv, tables, *, tm):
    b, s, d = x.shape
    c, sa, sb = tables
    pairs = MLA_HEADS // 2
    scale = (QK_NOPE + QK_ROPE) ** -0.5 * LOG2_E

    def full(arr):
        return pl.BlockSpec(arr.shape, lambda bi, i: (0,) * arr.ndim)

    tab = pl.BlockSpec((None, tm, LANES), lambda bi, i: (bi, i, 0))
    ops = [g.reshape(1, d), w_in, gq.reshape(1, -1), gkv.reshape(1, -1), wq_pad, wk_pad, wkr, wv]
    return pl.pallas_call(
        functools.partial(_mla_project_kernel, scale=scale),
        grid=(b, s // tm),
        in_specs=[pl.BlockSpec((None, tm, d), lambda bi, i: (bi, i, 0))] + [full(o) for o in ops]
                 + [tab, tab, tab],
        out_specs=[pl.BlockSpec((None, pairs, tm, 2 * LANES), lambda bi, i: (bi, 0, i, 0)),
                   pl.BlockSpec((None, pairs, tm, 2 * LANES), lambda bi, i: (bi, 0, i, 0)),
                   pl.BlockSpec((None, pairs, tm, 2 * LANES), lambda bi, i: (bi, 0, i, 0))],
        out_shape=[jax.ShapeDtypeStruct((b, pairs, s, 2 * LANES), BF16)] * 3,
        compiler_params=_params(("arbitrary", "arbitrary")),
        name="mla_project",
    )(x, *ops, c, sa, sb)


def _flash_kernel(q_ref, k_ref, v_ref, o_ref, s_scr, m_scr, acc_scr, *, tq, tk):
    nq = q_ref.shape[0] // tq
    nt = (((1,), (1,)), ((), ()))

    def query_rows(qi, rows):
        return pl.ds(pl.multiple_of(qi * tq, tq) + rows.start, rows.stop - rows.start)

    def key_rows(kj):
        return pl.ds(pl.multiple_of(kj * tk, tk), tk)

    def head_cols(half):
        return slice(half * LANES, (half + 1) * LANES)

    everything = slice(0, tq)
    top, bottom = slice(0, tk), slice(tk, tq)

    def scores(qi, kj, slot):
        ks = k_ref[key_rows(kj), :]
        for half in range(2):
            s_scr[slot, half] = lax.dot_general(q_ref[query_rows(qi, everything), head_cols(half)],
                                                ks[:, head_cols(half)], nt, preferred_element_type=F32)

    def update(kj, rows, s_pair, triangular):
        n = rows.stop - rows.start
        vs = v_ref[key_rows(kj), :]
        for half in range(2):
            s = s_pair[half]
            if triangular:
                row = lax.broadcasted_iota(jnp.int32, (n, tk), 0)
                col = lax.broadcasted_iota(jnp.int32, (n, tk), 1)
                s = jnp.where(row >= col, s, MASK_VALUE)
            m_prev = m_scr[half, rows]
            m_new = jnp.maximum(m_prev, jnp.max(s, axis=-1, keepdims=True))
            e = jnp.exp2(s - jnp.tile(m_new, (1, tk // LANES)))
            m_scr[half, rows] = m_new
            pv = jnp.dot(e.astype(BF16), vs[:, head_cols(half)], preferred_element_type=F32)
            acc_scr[half, rows] = acc_scr[half, rows] * jnp.exp2(m_prev - m_new) + pv

    def from_scratch(slot, rows):
        return [s_scr[slot, half, rows] for half in range(2)]

    def reset():
        m_scr[...] = jnp.full(m_scr.shape, MASK_VALUE, F32)
        acc_scr[...] = jnp.zeros(acc_scr.shape, F32)

    reset()
    scores(0, 0, 0)

    def query_tile(qi, carry):
        def body(jj, c):
            scores(qi, 2 * jj + 1, 1)
            update(2 * jj, everything, from_scratch(0, everything), False)
            scores(qi, 2 * jj + 2, 0)
            update(2 * jj + 1, everything, from_scratch(1, everything), False)
            return c

        unroll = 4

        def body_unrolled(ju, c):
            for u in range(unroll):
                body(unroll * ju + u, c)
            return c

        lax.fori_loop(0, qi // unroll, body_unrolled, 0)
        lax.fori_loop(unroll * (qi // unroll), qi, body, 0)
        update(2 * qi, top, from_scratch(0, top), True)
        update(2 * qi, bottom, from_scratch(0, bottom), False)
        ks = k_ref[key_rows(2 * qi + 1), :]
        last = [lax.dot_general(q_ref[query_rows(qi, bottom), head_cols(half)], ks[:, head_cols(half)], nt,
                                preferred_element_type=F32) for half in range(2)]
        update(2 * qi + 1, bottom, last, True)
        out = [acc_scr[half] / acc_scr[half][:, V_DIM:V_DIM + 1] for half in range(2)]
        low = lax.broadcasted_iota(jnp.int32, (tq, LANES), 1) < V_DIM
        o_ref[query_rows(qi, everything), :] = jnp.where(low, out[0], pltpu.roll(out[1], V_DIM, 1)).astype(o_ref.dtype)
        reset()
        scores(jnp.minimum(qi + 1, nq - 1), 0, 0)
        return carry

    lax.fori_loop(0, nq, query_tile, 0)


def _flash_attention(q, k, v, *, tk):
    b, pairs, s, _ = q.shape
    tq = 2 * tk
    resident = pl.Buffered(1)
    return pl.pallas_call(
        functools.partial(_flash_kernel, tq=tq, tk=tk),
        grid=(b, pairs),
        in_specs=[pl.BlockSpec((None, None, s, 2 * LANES), lambda bi, p: (bi, p, 0, 0), pipeline_mode=resident),
                  pl.BlockSpec((None, None, s, 2 * LANES), lambda bi, p: (bi, p, 0, 0), pipeline_mode=resident),
                  pl.BlockSpec((None, None, s, 2 * LANES), lambda bi, p: (bi, p, 0, 0), pipeline_mode=resident)],
        out_specs=pl.BlockSpec((None, s, LANES), lambda bi, p: (bi, 0, p)),
        out_shape=jax.ShapeDtypeStruct((b, s, pairs * LANES), BF16),
        scratch_shapes=[pltpu.VMEM((2, 2, tq, tk), F32),
                        pltpu.VMEM((2, tq, LANES), F32), pltpu.VMEM((2, tq, LANES), F32)],
        compiler_params=_params(("arbitrary", "arbitrary")),
        name="flash_attention",
    )(q, k, v)


def _route_kernel(x_ref, g_ref, wr_ref, meta_ref, h_ref, start_ref, cnt_ref, carry):
    i = pl.program_id(0)
    t = x_ref.shape[0]

    @pl.when(i == 0)
    def _():
        carry[...] = jnp.zeros(carry.shape, F32)

    h = _rms(x_ref[...], g_ref[...])
    h_ref[...] = h
    h_hi = h.astype(BF16)
    h_lo = (h - h_hi.astype(F32)).astype(BF16)
    logits = (jnp.dot(h_hi, wr_ref[0], preferred_element_type=F32)
              + jnp.dot(h_hi, wr_ref[1], preferred_element_type=F32)
              + jnp.dot(h_lo, wr_ref[0], preferred_element_type=F32))
    lane = lax.broadcasted_iota(jnp.int32, (t, LANES), 1)
    lg = jnp.where(lane < N_EXPERTS, logits, -jnp.inf)
    m1 = jnp.max(lg, axis=-1, keepdims=True)
    i1 = jnp.min(jnp.where(lg == m1, lane, LANES), axis=-1, keepdims=True)
    lg2 = jnp.where(lane == i1, -jnp.inf, lg)
    m2 = jnp.max(lg2, axis=-1, keepdims=True)
    i2 = jnp.min(jnp.where(lg2 == m2, lane, LANES), axis=-1, keepdims=True)
    e2 = jnp.exp(m2 - m1)
    w1 = 1.0 / (1.0 + e2)
    w2 = e2 / (1.0 + e2)
    hot1 = lane == i1
    hot2 = lane == i2
    onehot = (hot1 | hot2).astype(BF16)
    r = lax.broadcasted_iota(jnp.int32, (t, t), 0)
    c = lax.broadcasted_iota(jnp.int32, (t, t), 1)
    lower = (c < r).astype(BF16)
    before = jnp.dot(lower, onehot, preferred_element_type=F32) + carry[...]
    start_ref[...] = carry[...]
    rank1 = jnp.sum(jnp.where(hot1, before, 0.0), axis=-1, keepdims=True)
    rank2 = jnp.sum(jnp.where(hot2, before, 0.0), axis=-1, keepdims=True)
    carry[...] += jnp.sum(onehot.astype(F32), axis=0, keepdims=True)
    meta = jnp.zeros((t, LANES), F32)
    for k, val in enumerate((i1.astype(F32), i2.astype(F32), rank1, rank2, w1, w2)):
        meta = jnp.where(lane == k, val, meta)
    meta_ref[...] = meta
    cnt_ref[...] = carry[...]


def _route(x, g, w_router, *, tm):
    m, d = x.shape
    wr = jnp.zeros((d, LANES), F32).at[:, :N_EXPERTS].set(w_router)
    wr_hi = wr.astype(BF16)
    wr = jnp.stack([wr_hi, (wr - wr_hi.astype(F32)).astype(BF16)])
    return pl.pallas_call(
        _route_kernel,
        grid=(m // tm,),
        in_specs=[pl.BlockSpec((tm, d), lambda i: (i, 0)),
                  pl.BlockSpec((1, d), lambda i: (0, 0)),
                  pl.BlockSpec((2, d, LANES), lambda i: (0, 0, 0))],
        out_specs=[pl.BlockSpec((tm, LANES), lambda i: (i, 0)),
                   pl.BlockSpec((tm, d), lambda i: (i, 0)),
                   pl.BlockSpec((None, 1, LANES), lambda i: (i, 0, 0)),
                   pl.BlockSpec((1, LANES), lambda i: (0, 0))],
        out_shape=[jax.ShapeDtypeStruct((m, LANES), F32),
                   jax.ShapeDtypeStruct((m, d), F32),
                   jax.ShapeDtypeStruct((m // tm, 1, LANES), F32),
                   jax.ShapeDtypeStruct((1, LANES), F32)],
        scratch_shapes=[pltpu.VMEM((1, LANES), F32)],
        compiler_params=_params(("arbitrary",)),
        name="route",
    )(x, g.reshape(1, d), wr)


def _invert_kernel(start_ref, base_ref, meta_ref, init_ref, list_ref, obuf, sem):
    del init_ref
    s = pl.program_id(0)
    t = meta_ref.shape[0]
    meta = meta_ref[...]
    second_expert = meta[:, 1:2]
    by_lane = meta.T
    i1, i2, r1, r2 = by_lane[0:1], by_lane[1:2], by_lane[2:3], by_lane[3:4]
    token = lax.broadcasted_iota(jnp.int32, (t, 1), 0) + s * t
    lane = lax.broadcasted_iota(jnp.int32, (t, LANES), 1)
    place = lax.broadcasted_iota(jnp.int32, (t, t), 0).astype(F32)
    copies = []
    for e in range(N_EXPERTS):
        first, second = i1 == e, i2 == e
        start = start_ref[s * N_EXPERTS + e]
        local = jnp.where(first, r1, r2) - start.astype(F32)
        onehot = ((local == place) & (first | second)).astype(BF16)
        code = 2 * token + (second_expert == e).astype(jnp.int32) + 1
        pieces = jnp.where(lane == 0, (code >> 8).astype(F32),
                           jnp.where(lane == 1, (code & 255).astype(F32), 0.0)).astype(BF16)
        obuf[e] = jnp.dot(onehot, pieces, preferred_element_type=F32)
        copy = pltpu.make_async_copy(obuf.at[e], list_ref.at[pl.ds(base_ref[e] + start, t)], sem)
        copy.start()
        copies.append(copy)
    for copy in copies:
        copy.wait()


def _invert(meta, starts, base, *, tm, rows):
    m = meta.shape[0]
    grid_spec = pltpu.PrefetchScalarGridSpec(
        num_scalar_prefetch=2,
        grid=(m // tm,),
        in_specs=[pl.BlockSpec((tm, LANES), lambda i, st, ba: (i, 0)),
                  pl.BlockSpec(memory_space=pl.ANY)],
        out_specs=pl.BlockSpec(memory_space=pl.ANY),
        scratch_shapes=[pltpu.VMEM((N_EXPERTS, tm, LANES), F32), pltpu.SemaphoreType.DMA(())],
    )
    return pl.pallas_call(
        _invert_kernel,
        grid_spec=grid_spec,
        out_shape=jax.ShapeDtypeStruct((rows, LANES), F32),
        input_output_aliases={3: 0},
        compiler_params=_params(("arbitrary",)),
        name="invert",
    )(starts, base, meta, jnp.zeros((rows, LANES), F32))


def _grouped_kernel(te_ref, tv_ref, tb_ref, tp_ref, cur_ref, nxt_ref, dprev_ref, x_hbm, wg_ref, wu_ref, wd_ref,
                    y_hbm, xbuf, hbuf, ybuf, gsem, ssem):
    del te_ref, tb_ref, tp_ref
    i = pl.program_id(0)
    last = pl.num_programs(0) - 1
    tm = xbuf.shape[1]
    slot = i % 2
    other = 1 - slot

    def gather_row(ids_ref, j, buf):
        return pltpu.make_async_copy(x_hbm.at[pl.ds(ids_ref[j], 1)], xbuf.at[buf, pl.ds(j, 1)], gsem.at[buf])

    def scatter_row(j, buf):
        return pltpu.make_async_copy(ybuf.at[buf, pl.ds(j, 1)], y_hbm.at[pl.ds(dprev_ref[j], 1)], ssem.at[buf])

    def gather_looped(ids_ref, buf):
        def body(j, carry):
            gather_row(ids_ref, j, buf).start()
            return carry
        lax.fori_loop(0, tm, body, 0, unroll=8)

    def scatter_looped(buf):
        def body(j, carry):
            scatter_row(j, buf).start()
            return carry
        lax.fori_loop(0, tm, body, 0, unroll=8)

    @pl.when(i == 0)
    def _():
        ybuf[...] = jnp.zeros(ybuf.shape, F32)
        gather_looped(cur_ref, 0)

    pltpu.make_async_copy(x_hbm.at[pl.ds(0, tm)], xbuf.at[slot], gsem.at[slot]).wait()

    @pl.when(i >= 1)
    def _():
        pltpu.make_async_copy(ybuf.at[slot], y_hbm.at[pl.ds(0, tm)], ssem.at[slot]).wait()

    @pl.when(tv_ref[i] > 0)
    def _():
        hbuf[...] = xbuf[slot].astype(BF16)
        for j in range(tm):
            gather_row(nxt_ref, j, other).start(priority=0)
            scatter_row(j, other).start(priority=1)
        h = hbuf[...]
        gate = jnp.dot(h, wg_ref[...], preferred_element_type=F32)
        up = jnp.dot(h, wu_ref[...], preferred_element_type=F32)
        act = (gate * jax.nn.sigmoid(gate) * up).astype(BF16)
        ybuf[slot] = jnp.dot(act, wd_ref[...], preferred_element_type=F32)

    @pl.when(tv_ref[i] == 0)
    def _():
        gather_looped(nxt_ref, other)
        scatter_looped(other)

    @pl.when(i == last)
    def _():
        pltpu.make_async_copy(x_hbm.at[pl.ds(0, tm)], xbuf.at[other], gsem.at[other]).wait()
        pltpu.make_async_copy(ybuf.at[other], y_hbm.at[pl.ds(0, tm)], ssem.at[other]).wait()


def _grouped_swiglu(x, src, dst, tile_expert, tile_valid, tile_block, prev_block, wg, wu, wd, *, tm, y_rows):
    m, d = x.shape
    ff = wg.shape[2]
    n_tiles = tile_expert.shape[0]
    resident = pl.Buffered(1)
    smem_block = functools.partial(pl.BlockSpec, (tm,), memory_space=pltpu.SMEM)
    grid_spec = pltpu.PrefetchScalarGridSpec(
        num_scalar_prefetch=4,
        grid=(n_tiles,),
        in_specs=[smem_block(lambda i, te, tv, tb, tp: (tb[i],)),
                  smem_block(lambda i, te, tv, tb, tp: (tb[jnp.minimum(i + 1, n_tiles - 1)],)),
                  smem_block(lambda i, te, tv, tb, tp: (tp[i],)),
                  pl.BlockSpec(memory_space=pl.ANY),
                  pl.BlockSpec((None, d, ff), lambda i, te, tv, tb, tp: (te[i], 0, 0), pipeline_mode=resident),
                  pl.BlockSpec((None, d, ff), lambda i, te, tv, tb, tp: (te[i], 0, 0), pipeline_mode=resident),
                  pl.BlockSpec((None, ff, d), lambda i, te, tv, tb, tp: (te[i], 0, 0), pipeline_mode=resident)],
        out_specs=pl.BlockSpec(memory_space=pl.ANY),
        scratch_shapes=[pltpu.VMEM((2, tm, d), F32), pltpu.VMEM((tm, d), BF16), pltpu.VMEM((2, tm, d), F32),
                        pltpu.SemaphoreType.DMA((2,)), pltpu.SemaphoreType.DMA((2,))],
    )
    return pl.pallas_call(
        _grouped_kernel,
        grid_spec=grid_spec,
        out_shape=jax.ShapeDtypeStruct((y_rows, d), F32),
        compiler_params=_params(("arbitrary",), vmem=GROUPED_VMEM_LIMIT),
        name="grouped_swiglu",
    )(tile_expert, tile_valid, tile_block, prev_block, src, src, dst, x, wg, wu, wd)


def _combine_kernel(x_ref, meta_ref, g_ref, y0_ref, y1_ref, o_ref):
    meta = meta_ref[...]
    x = x_ref[...] + (meta[:, 4:5] * y0_ref[...] + meta[:, 5:6] * y1_ref[...])
    o_ref[...] = _rms(x, g_ref[...])


def _combine(x, meta, y, g, *, tm):
    m, d = x.shape
    return pl.pallas_call(
        _combine_kernel,
        grid=(m // tm,),
        in_specs=[pl.BlockSpec((tm, d), lambda i: (i, 0)),
                  pl.BlockSpec((tm, LANES), lambda i: (i, 0)),
                  pl.BlockSpec((1, d), lambda i: (0, 0)),
                  pl.BlockSpec((tm, d), lambda i: (i, 0)),
                  pl.BlockSpec((tm, d), lambda i: (m // tm + i, 0))],
        out_specs=pl.BlockSpec((tm, d), lambda i: (i, 0)),
        out_shape=jax.ShapeDtypeStruct((m, d), F32),
        compiler_params=_params(("arbitrary",)),
        name="combine",
    )(x, meta, g.reshape(1, d), y, y)


def _moe_layer(x, g, w_router, wg, wu, wd, final_g, *, tm):
    m, d = x.shape
    route_tm = 512
    meta, hn, starts, counts = _route(x, g, w_router, tm=route_tm)
    cnt = counts[0, :N_EXPERTS].astype(jnp.int32)
    tiles = (cnt + tm - 1) // tm
    tile_end = jnp.cumsum(tiles)
    first_tile = tile_end - tiles
    n_tiles = (2 * m) // tm + N_EXPERTS
    tile_id = jnp.arange(n_tiles, dtype=jnp.int32)
    tile_expert = jnp.sum(tile_id[:, None] >= tile_end[None, :], axis=1).astype(jnp.int32)
    tile_valid = (tile_expert < N_EXPERTS).astype(jnp.int32)
    tile_expert = jnp.minimum(tile_expert, N_EXPERTS - 1)
    assert route_tm <= tm
    base = (first_tile + jnp.arange(N_EXPERTS, dtype=jnp.int32)) * tm
    list_blocks = n_tiles + N_EXPERTS
    lists = _invert(meta, starts[:, 0, :N_EXPERTS].astype(jnp.int32).reshape(-1), base.astype(jnp.int32),
                    tm=route_tm, rows=list_blocks * tm)
    code = (lists[:, 0] * 256.0 + lists[:, 1]).astype(jnp.int32) - 1
    row_in_tile = jnp.tile(jnp.arange(tm, dtype=jnp.int32), list_blocks)
    src = jnp.where(code < 0, row_in_tile, code >> 1)
    dst = jnp.where(code < 0, 2 * m + row_in_tile, (code & 1) * m + (code >> 1))
    empty_block = list_blocks - 1
    tile_block = tile_id + tile_expert
    dst_block = jnp.where(tile_valid > 0, tile_block, empty_block).astype(jnp.int32)
    prev_block = jnp.concatenate([jnp.full((1,), empty_block, jnp.int32), dst_block[:-1]])
    tile_block = jnp.where(tile_valid > 0, tile_block, empty_block).astype(jnp.int32)
    y = _grouped_swiglu(hn, src, dst, tile_expert, tile_valid, tile_block, prev_block, wg, wu, wd, tm=tm,
                        y_rows=2 * m + tm)
    return _combine(x, meta, y, final_g, tm=512)


def _mla_weights(w_q_up, w_kv_up):
    hd = QK_NOPE + QK_ROPE
    wq = w_q_up.reshape(Q_LORA, MLA_HEADS, hd)
    wq_pad = jnp.pad(wq, ((0, 0), (0, 0), (0, LANES - hd))).reshape(Q_LORA, MLA_HEADS * LANES)
    wkv = w_kv_up.reshape(KV_LORA, MLA_HEADS, QK_NOPE + V_DIM)
    wk_pad = jnp.pad(wkv[:, :, :QK_NOPE], ((0, 0), (0, 0), (0, LANES - QK_NOPE)))
    wk_pad = wk_pad.reshape(KV_LORA, MLA_HEADS * LANES)
    wv = jnp.pad(wkv[:, :, QK_NOPE:], ((0, 0), (0, 0), (0, LANES - V_DIM))).reshape(KV_LORA, MLA_HEADS * LANES)
    place = jnp.zeros((QK_ROPE, LANES), F32).at[jnp.arange(QK_ROPE), QK_NOPE + jnp.arange(QK_ROPE)].set(1.0)
    return wq_pad.astype(BF16), wk_pad.astype(BF16), place.astype(BF16), wv.astype(BF16)


def _rope_lane_tables(positions):
    half = QK_ROPE // 2
    inv = ROPE_THETA ** (-jnp.arange(0, QK_ROPE, 2, dtype=F32) / QK_ROPE)
    ang = positions.astype(F32)[..., None] * inv
    cos, sin = jnp.cos(ang), jnp.sin(ang)
    shape = positions.shape
    ones = jnp.ones(shape + (QK_NOPE,), F32)
    zeros_n = jnp.zeros(shape + (QK_NOPE,), F32)
    zeros_h = jnp.zeros(shape + (half,), F32)
    pad = jnp.zeros(shape + (LANES - QK_NOPE - QK_ROPE,), F32)
    c = jnp.concatenate([ones, cos, cos, pad], axis=-1)
    sa = jnp.concatenate([zeros_n, -sin, zeros_h, pad], axis=-1)
    sb = jnp.concatenate([zeros_n, zeros_h, sin, pad], axis=-1)
    return c, sa, sb


def kernel(x, positions, mix_norm_g, ffn_norm_g, a_w_qkv, a_w_out, b_w_in, b_q_norm_g, b_kv_norm_g,
           b_w_q_up, b_w_kv_up, b_w_out, ffn_w_gate, ffn_w_up, ffn_w_down, moe_w_router, moe_w_gate,
           moe_w_up, moe_w_down, final_norm_g):
    b, s, d = x.shape
    m = b * s
    xf = x.reshape(m, d)

    group_cols = 3 * DSA_WIDTH
    q_scale = jnp.where(jnp.arange(a_w_qkv.shape[2]) % group_cols < DSA_WIDTH, DSA_HEAD_DIM ** -0.5, 1.0)
    w_qkv = (a_w_qkv[0] * q_scale).astype(BF16)
    state = None
    for group, (_, dil) in enumerate(DIL_PAIRS):
        w_group = w_qkv[:, group * group_cols:(group + 1) * group_cols]
        qkv_v = _qkv_project(xf, mix_norm_g[0], w_group, dil=dil, tm=1024 if dil == 1 else 512)
        qkv_v = qkv_v.reshape(b, s // dil, dil * group_cols)
        state = _merge_group(qkv_v, positions, state, group=group, qb=4)
    x1 = _unpermute_project(xf, state, a_w_out[0].astype(BF16), dil=DIL_PAIRS[-1][1], tm=512)
    x2 = _swiglu(x1, ffn_norm_g[0], ffn_w_gate[0].astype(BF16), ffn_w_up[0].astype(BF16),
                 ffn_w_down[0].astype(BF16), tm=512)

    wq_pad, wk_pad, wkr, wv = _mla_weights(b_w_q_up[0], b_w_kv_up[0])
    q, k, v = _mla_project(x2.reshape(b, s, d), mix_norm_g[1], b_w_in[0].astype(BF16), b_q_norm_g[0],
                           b_kv_norm_g[0], wq_pad, wk_pad, wkr, wv, _rope_lane_tables(positions), tm=512)
    attn = _flash_attention(q, k, v, tk=512)
    x3 = _matmul_residual(x2, attn.reshape(m, MLA_HEADS * V_DIM), b_w_out[0].astype(BF16), tm=1024)
    out = _moe_layer(x3, ffn_norm_g[1], moe_w_router[0], moe_w_gate[0].astype(BF16),
                     moe_w_up[0].astype(BF16), moe_w_down[0].astype(BF16), final_norm_g,
                     tm=512)
    return out.reshape(b, s, d)
```
